```python
import jax, jax.numpy as jnp
from jax import lax
import numpy as np

D_MODEL = 2048
BATCH = 4
SEQ = 8192
DEPTH = 2

N_MIXERS = 2
N_GLA_LAYERS = (DEPTH + 1) // 2
N_SGU_LAYERS = DEPTH // 2

GLA_HEADS = 4
GLA_DK = D_MODEL // 2
GLA_DV = D_MODEL
GLA_DK_HEAD = GLA_DK // GLA_HEADS
GLA_DV_HEAD = GLA_DV // GLA_HEADS
GLA_RANK = 16
GLA_TAU = 16.0
GLA_CHUNK = 64
GLA_IN = 2 * GLA_DK + 2 * GLA_DV + GLA_RANK

SGU_CHUNK = 128
SGU_GROUPS = 8
SGU_HALF = 2 * D_MODEL
SGU_GROUP_W = SGU_HALF // SGU_GROUPS

MOE_GROUPS = 8
MOE_PER_GROUP = 8
MOE_EXPERTS = MOE_GROUPS * MOE_PER_GROUP
MOE_TOPK = 2
MOE_HIDDEN = D_MODEL // 4
MOE_BLOCK = 128

N_MOD = 6
EPS = 1e-6

kernel_name = "hybrid_gla_sgu_hmoe_adaln"


def rms_norm(x, gain):
    xf = x.astype(jnp.float32)
    y = xf * lax.rsqrt(jnp.mean(xf * xf, axis=-1, keepdims=True) + EPS)
    return (y * gain.astype(jnp.float32)).astype(x.dtype)


def layer_norm(x, gain, bias):
    xf = x.astype(jnp.float32)
    mu = jnp.mean(xf, axis=-1, keepdims=True)
    var = jnp.mean(jnp.square(xf - mu), axis=-1, keepdims=True)
    y = (xf - mu) * lax.rsqrt(var + EPS) * gain.astype(jnp.float32) + bias.astype(jnp.float32)
    return y.astype(x.dtype)


def modulate(h, shift, scale):
    return h * (1 + scale[:, None, :]) + shift[:, None, :]


def gla_chunk_scan(q, k, v, log_a):
    B, H, S, dk = q.shape
    dv = v.shape[-1]
    nc = S // GLA_CHUNK

    def to_chunks(t):
        return jnp.moveaxis(t.reshape(B, H, nc, GLA_CHUNK, t.shape[-1]), 2, 0)

    causal = jnp.tril(jnp.ones((GLA_CHUNK, GLA_CHUNK), dtype=bool))

    def step(state, inp):
        qc, kc, vc, gc = inp
        b = jnp.cumsum(gc, axis=2)
        o_inter = jnp.einsum('bhtc,bhcv->bhtv', qc * jnp.exp(b), state)
        rel = jnp.where(causal[None, None, :, :, None],
                        b[:, :, :, None, :] - b[:, :, None, :, :], -jnp.inf)
        scores = jnp.einsum('bhtc,bhsc,bhtsc->bhts', qc, kc, jnp.exp(rel))
        o_intra = jnp.einsum('bhts,bhsv->bhtv', scores, vc)
        b_last = b[:, :, -1:, :]
        state = (jnp.exp(b_last[:, :, 0, :])[..., None] * state
                 + jnp.einsum('bhsc,bhsv->bhcv', kc * jnp.exp(b_last - b), vc))
        return state, o_inter + o_intra

    s0 = jnp.zeros((B, H, dk, dv), jnp.float32)
    _, o = lax.scan(step, s0, (to_chunks(q), to_chunks(k), to_chunks(v), to_chunks(log_a)))
    return jnp.moveaxis(o, 0, 2).reshape(B, H, S, dv)


def gla_mixer(h, w_in, b_r, w_a_up, b_a, norm_g, w_out):
    B, S, _ = h.shape
    proj = h @ w_in
    q, k, v, r, a_low = jnp.split(
        proj, [GLA_DK, 2 * GLA_DK, 2 * GLA_DK + GLA_DV, 2 * GLA_DK + 2 * GLA_DV], axis=-1)
    log_a = jax.nn.log_sigmoid((a_low @ w_a_up + b_a).astype(jnp.float32)) / GLA_TAU

    def heads(t):
        return t.reshape(B, S, GLA_HEADS, -1).transpose(0, 2, 1, 3).astype(jnp.float32)

    o = gla_chunk_scan(heads(q) * (GLA_DK_HEAD ** -0.5), heads(k), heads(v), heads(log_a))
    o = rms_norm(o, norm_g)
    o = o.transpose(0, 2, 1, 3).reshape(B, S, GLA_DV).astype(h.dtype)
    return (jax.nn.silu(r + b_r) * o) @ w_out


def sgu_mixer(h, w_in, b_in, ln_g, ln_b, w_s, b_s, w_out):
    B, S, _ = h.shape
    z = jax.nn.gelu(h @ w_in + b_in)
    u, v = jnp.split(z, 2, axis=-1)
    v = layer_norm(v, ln_g, ln_b)
    nc = S // SGU_CHUNK
    v = v.reshape(B, nc, SGU_CHUNK, SGU_GROUPS, SGU_GROUP_W)
    w_causal = jnp.where(jnp.tril(jnp.ones((SGU_CHUNK, SGU_CHUNK), dtype=bool))[None], w_s, 0.0)
    mixed = jnp.einsum('gts,bnsgc->bntgc', w_causal, v) + b_s.T[:, :, None]
    return (u * mixed.reshape(B, S, SGU_HALF)) @ w_out


def hier_moe(h, w_grp, b_grp, w_exp, b_exp, w1, w3, w2):
    B, S, D = h.shape
    T = B * S
    xf = h.reshape(T, D)
    grp_p = jax.nn.softmax((xf @ w_grp + b_grp).astype(jnp.float32), axis=-1)
    g_w, g_idx = lax.top_k(grp_p, 1)
    exp_logits = (xf @ w_exp + b_exp).astype(jnp.float32).reshape(T, MOE_GROUPS, MOE_PER_GROUP)
    idx = jnp.broadcast_to(g_idx[:, :, None], (T, 1, MOE_PER_GROUP))
    in_grp = jnp.take_along_axis(exp_logits, idx, axis=1)[:, 0]
    e_w, e_idx = lax.top_k(jax.nn.softmax(in_grp, axis=-1), MOE_TOPK)
    gate = g_w * (e_w / jnp.sum(e_w, axis=-1, keepdims=True))
    expert = g_idx * MOE_PER_GROUP + e_idx

    A = T * MOE_TOPK
    e_flat = expert.reshape(A)
    tok_flat = jnp.repeat(jnp.arange(T, dtype=jnp.int32), MOE_TOPK)
    w_flat = gate.reshape(A)
    order = jnp.argsort(e_flat)
    e_sorted = e_flat[order]
    counts = jnp.zeros((MOE_EXPERTS,), jnp.int32).at[e_flat].add(1)
    padded = (counts + MOE_BLOCK - 1) // MOE_BLOCK * MOE_BLOCK
    starts = jnp.cumsum(counts) - counts
    pad_starts = jnp.cumsum(padded) - padded
    pad_ends = pad_starts + padded
    dest = pad_starts[e_sorted] + (jnp.arange(A, dtype=jnp.int32) - starts[e_sorted])
    n_blocks = -(-(A + MOE_EXPERTS * (MOE_BLOCK - 1)) // MOE_BLOCK)
    P = n_blocks * MOE_BLOCK
    buf_tok = jnp.full((P,), T, jnp.int32).at[dest].set(tok_flat[order])
    buf_w = jnp.zeros((P,), jnp.float32).at[dest].set(w_flat[order])
    block_exp = jnp.minimum(
        jnp.searchsorted(pad_ends, jnp.arange(n_blocks, dtype=jnp.int32) * MOE_BLOCK, side='right'),
        MOE_EXPERTS - 1)
    x_pad = jnp.concatenate([xf, jnp.zeros((1, D), xf.dtype)], axis=0)

    def run_block(args):
        tok, wt, e = args
        xb = x_pad[tok]
        hb = jax.nn.silu(xb @ w1[e]) * (xb @ w3[e])
        return (hb @ w2[e]) * wt[:, None].astype(xb.dtype)

    y_blocks = lax.map(run_block, (buf_tok.reshape(n_blocks, MOE_BLOCK),
                                   buf_w.reshape(n_blocks, MOE_BLOCK), block_exp))
    y = jnp.zeros((T + 1, D), h.dtype).at[buf_tok].add(y_blocks.reshape(P, D))
    return y[:T].reshape(B, S, D)


def setup_inputs(seed: int = 0) -> dict:
    key = jax.random.key(seed)
    ks = jax.random.split(key, 27)
    n = jax.random.normal
    f32 = jnp.float32
    D = D_MODEL
    return {
        "x": n(ks[0], (BATCH, SEQ, D), f32),
        "c": n(ks[1], (BATCH, D), f32),
        "norm_mix": 1.0 + 0.02 * n(ks[2], (DEPTH, D), f32),
        "norm_ffn": 1.0 + 0.02 * n(ks[3], (DEPTH, D), f32),
        "ada_w": n(ks[4], (DEPTH, D, N_MOD * D), f32) * D ** -0.5,
        "ada_b": 0.02 * n(ks[5], (DEPTH, N_MOD * D), f32),
        "gla_w_in": n(ks[6], (N_GLA_LAYERS, D, GLA_IN), f32) * D ** -0.5,
        "gla_b_r": 0.02 * n(ks[7], (N_GLA_LAYERS, GLA_DV), f32),
        "gla_w_a_up": n(ks[8], (N_GLA_LAYERS, GLA_RANK, GLA_DK), f32) * GLA_RANK ** -0.5,
        "gla_b_a": 0.1 * n(ks[9], (N_GLA_LAYERS, GLA_DK), f32),
        "gla_norm_g": 1.0 + 0.02 * n(ks[10], (N_GLA_LAYERS, GLA_DV_HEAD), f32),
        "gla_w_out": n(ks[11], (N_GLA_LAYERS, GLA_DV, D), f32) * GLA_DV ** -0.5,
        "sgu_w_in": n(ks[12], (N_SGU_LAYERS, D, 2 * SGU_HALF), f32) * D ** -0.5,
        "sgu_b_in": 0.02 * n(ks[13], (N_SGU_LAYERS, 2 * SGU_HALF), f32),
        "sgu_ln_g": 1.0 + 0.02 * n(ks[14], (N_SGU_LAYERS, SGU_HALF), f32),
        "sgu_ln_b": 0.02 * n(ks[15], (N_SGU_LAYERS, SGU_HALF), f32),
        "sgu_w_s": n(ks[16], (N_SGU_LAYERS, SGU_GROUPS, SGU_CHUNK, SGU_CHUNK), f32) * SGU_CHUNK ** -0.5,
        "sgu_b_s": 1.0 + 0.02 * n(ks[17], (N_SGU_LAYERS, SGU_GROUPS, SGU_CHUNK), f32),
        "sgu_w_out": n(ks[18], (N_SGU_LAYERS, SGU_HALF, D), f32) * SGU_HALF ** -0.5,
        "moe_w_grp": n(ks[19], (DEPTH, D, MOE_GROUPS), f32) * D ** -0.5,
        "moe_b_grp": 0.01 * n(ks[20], (DEPTH, MOE_GROUPS), f32),
        "moe_w_exp": n(ks[21], (DEPTH, D, MOE_EXPERTS), f32) * D ** -0.5,
        "moe_b_exp": 0.01 * n(ks[22], (DEPTH, MOE_EXPERTS), f32),
        "moe_w1": n(ks[23], (DEPTH, MOE_EXPERTS, D, MOE_HIDDEN), f32) * D ** -0.5,
        "moe_w3": n(ks[24], (DEPTH, MOE_EXPERTS, D, MOE_HIDDEN), f32) * D ** -0.5,
        "moe_w2": n(ks[25], (DEPTH, MOE_EXPERTS, MOE_HIDDEN, D), f32) * MOE_HIDDEN ** -0.5,
        "final_norm": 1.0 + 0.02 * n(ks[26], (D,), f32),
    }


def reference(x, c, norm_mix, norm_ffn, ada_w, ada_b,
              gla_w_in, gla_b_r, gla_w_a_up, gla_b_a, gla_norm_g, gla_w_out,
              sgu_w_in, sgu_b_in, sgu_ln_g, sgu_ln_b, sgu_w_s, sgu_b_s, sgu_w_out,
              moe_w_grp, moe_b_grp, moe_w_exp, moe_b_exp, moe_w1, moe_w3, moe_w2,
              final_norm):
    c_act = jax.nn.silu(c)
    for i in range(DEPTH):
        mod = c_act @ ada_w[i] + ada_b[i]
        sh1, sc1, g1, sh2, sc2, g2 = jnp.split(mod, N_MOD, axis=-1)
        h = modulate(rms_norm(x, norm_mix[i]), sh1, sc1)
        j = i // N_MIXERS
        if i % N_MIXERS == 0:
            y = gla_mixer(h, gla_w_in[j], gla_b_r[j], gla_w_a_up[j], gla_b_a[j],
                          gla_norm_g[j], gla_w_out[j])
        else:
            y = sgu_mixer(h, sgu_w_in[j], sgu_b_in[j], sgu_ln_g[j], sgu_ln_b[j],
                          sgu_w_s[j], sgu_b_s[j], sgu_w_out[j])
        x = x + g1[:, None, :] * y
        h = modulate(rms_norm(x, norm_ffn[i]), sh2, sc2)
        y = hier_moe(h, moe_w_grp[i], moe_b_grp[i], moe_w_exp[i], moe_b_exp[i],
                     moe_w1[i], moe_w3[i], moe_w2[i])
        x = x + g2[:, None, :] * y
    return rms_norm(x, final_norm)
```

```python
import functools
import math

import jax
import jax.numpy as jnp
from jax import lax
from jax.experimental import pallas as pl
from jax.experimental.pallas import tpu as pltpu

F32 = jnp.float32
BF16 = jnp.bfloat16

EPS = 1e-6
N_MOD = 6
GLA_HEADS = 4
GLA_RANK = 16
GLA_TAU = 16.0
GLA_CHUNK = 128
GLA_FLAT = 8
SGU_CHUNK = 128
SGU_GROUPS = 8
MOE_GROUPS = 8
MOE_PER_GROUP = 8
MOE_EXPERTS = MOE_GROUPS * MOE_PER_GROUP
MOE_TOPK = 2
MOE_BLOCK = 128
LANES = 128
V7X_VMEM_LIMIT_BYTES = 56 * 2**20


def _params(*semantics):
    return pltpu.CompilerParams(dimension_semantics=semantics,
                                vmem_limit_bytes=V7X_VMEM_LIMIT_BYTES)


def _dot(a, b):
    return jnp.dot(a, b, preferred_element_type=F32)


def _dot_nt(a, b):
    return lax.dot_general(a, b, (((1,), (1,)), ((), ())), preferred_element_type=F32)


def _dot_tn(a, b):
    return lax.dot_general(a, b, (((0,), (0,)), ((), ())), preferred_element_type=F32)


def _split3(x):
    hi = x.astype(BF16)
    r1 = x - hi.astype(F32)
    mid = r1.astype(BF16)
    lo = (r1 - mid.astype(F32)).astype(BF16)
    return hi, mid, lo


def _sigmoid(x):
    return 1.0 / (1.0 + jnp.exp(-x))


def _gelu_tanh(x):
    c = math.sqrt(2.0 / math.pi)
    return x * (0.5 * (1.0 + jnp.tanh(c * (x + 0.044715 * (x * x * x)))))


def _pick(n, pref):
    t = min(n, pref)
    while n % t:
        t //= 2
    return t


def _adaln_kernel(c_ref, w_ref, b_ref, o_ref):
    c = c_ref[...]
    ca = c * _sigmoid(c)
    hi, mid, _ = _split3(ca)
    w = w_ref[0].astype(BF16)
    o_ref[0] = _dot(hi, w) + _dot(mid, w) + b_ref[0]


def _adaln(c, ada_w, ada_b):
    depth, d, n = ada_w.shape
    b = c.shape[0]
    rows = 8
    cp = jnp.zeros((rows, d), F32).at[:b].set(c)
    tn = _pick(n, 1024)
    out = pl.pallas_call(
        _adaln_kernel,
        grid=(depth, n // tn),
        in_specs=[pl.BlockSpec((rows, d), lambda l, j: (0, 0)),
                  pl.BlockSpec((1, d, tn), lambda l, j: (l, 0, j)),
                  pl.BlockSpec((1, 1, tn), lambda l, j: (l, 0, j))],
        out_specs=pl.BlockSpec((1, rows, tn), lambda l, j: (l, 0, j)),
        out_shape=jax.ShapeDtypeStruct((depth, rows, n), F32),
        compiler_params=_params("parallel", "parallel"),
        name="adaln",
    )(cp, ada_w, ada_b.reshape(depth, 1, n))
    return out[:, :b]


def _norm_mod(x, gain, shift, scale):
    ms = jnp.mean(x * x, axis=-1, keepdims=True)
    return (x * lax.rsqrt(ms + EPS) * gain) * (1.0 + scale) + shift


def _norm_matmul_kernel(x_ref, g_ref, sh_ref, sc_ref, w_ref, *rest, gelu):
    if gelu:
        bias_ref, o_ref, h_scr = rest
    else:
        o_ref, h_scr = rest

    @pl.when(pl.program_id(1) == 0)
    def _():
        h_scr[...] = _norm_mod(x_ref[...], g_ref[...], sh_ref[0], sc_ref[0]).astype(BF16)

    acc = _dot(h_scr[...], w_ref[...])
    if gelu:
        acc = _gelu_tanh(acc + bias_ref[...])
    o_ref[...] = acc.astype(o_ref.dtype)


def _norm_matmul(x, gain, shift, scale, w, bias, out_dtype, rows_per_batch, name, tm=1024, tn=1024):
    t, d = x.shape
    n = w.shape[1]
    tm = _pick(rows_per_batch, tm)
    tn = _pick(n, tn)
    tpb = rows_per_batch // tm
    bsz = shift.shape[0]
    in_specs = [pl.BlockSpec((tm, d), lambda i, j: (i, 0)),
                pl.BlockSpec((1, d), lambda i, j: (0, 0)),
                pl.BlockSpec((1, 1, d), lambda i, j: (i // tpb, 0, 0)),
                pl.BlockSpec((1, 1, d), lambda i, j: (i // tpb, 0, 0)),
                pl.BlockSpec((d, tn), lambda i, j: (0, j))]
    args = [x, gain.reshape(1, d), shift.reshape(bsz, 1, d), scale.reshape(bsz, 1, d), w]
    if bias is not None:
        in_specs.append(pl.BlockSpec((1, tn), lambda i, j: (0, j)))
        args.append(bias.reshape(1, n))
    return pl.pallas_call(
        functools.partial(_norm_matmul_kernel, gelu=bias is not None),
        grid=(t // tm, n // tn),
        in_specs=in_specs,
        out_specs=pl.BlockSpec((tm, tn), lambda i, j: (i, j)),
        out_shape=jax.ShapeDtypeStruct((t, n), out_dtype),
        scratch_shapes=[pltpu.VMEM((tm, d), BF16)],
        compiler_params=_params("parallel", "arbitrary"),
        name=name,
    )(*args)


def _matmul_residual_kernel(a_ref, w_ref, x_ref, gate_ref, o_ref):
    o_ref[...] = x_ref[...] + gate_ref[0] * _dot(a_ref[...], w_ref[...])


def _matmul_residual(a, w, x, gate, rows_per_batch, name, tm=1024, tn=1024):
    t, k = a.shape
    n = w.shape[1]
    tm = _pick(rows_per_batch, tm)
    tn = _pick(n, tn)
    tpb = rows_per_batch // tm
    bsz = gate.shape[0]
    return pl.pallas_call(
        _matmul_residual_kernel,
        grid=(t // tm, n // tn),
        in_specs=[pl.BlockSpec((tm, k), lambda i, j: (i, 0)),
                  pl.BlockSpec((k, tn), lambda i, j: (0, j)),
                  pl.BlockSpec((tm, tn), lambda i, j: (i, j)),
                  pl.BlockSpec((1, 1, tn), lambda i, j: (i // tpb, 0, j))],
        out_specs=pl.BlockSpec((tm, tn), lambda i, j: (i, j)),
        out_shape=jax.ShapeDtypeStruct((t, n), F32),
        compiler_params=_params("parallel", "parallel"),
        name=name,
    )(a, w, x, gate.reshape(bsz, 1, n))


def _rows_bcast(b, n, off):
    c, dk = b.shape
    parts = [jnp.broadcast_to(b[j * n + off:j * n + off + 1, :], (n, dk)) for j in range(c // n)]
    return parts[0] if len(parts) == 1 else jnp.concatenate(parts, axis=0)


def _gla_scan_kernel(q_ref, k_ref, v_ref, r_ref, al_ref, wup_ref, ba_ref, br_ref, ng_ref,
                     o_ref, st_ref, *, scale):
    c, dk = q_ref.shape

    @pl.when(pl.program_id(2) == 0)
    def _():
        st_ref[...] = jnp.zeros_like(st_ref)

    q = q_ref[...].astype(F32) * scale
    k = k_ref[...].astype(F32)
    v = v_ref[...]

    al = _split3(al_ref[:, :GLA_RANK])
    wu = _split3(wup_ref[...])
    pre = (_dot(al[0], wu[0]) + _dot(al[0], wu[1]) + _dot(al[1], wu[0])
           + _dot(al[1], wu[1]) + _dot(al[0], wu[2]) + _dot(al[2], wu[0])) + ba_ref[...]
    g = (jnp.minimum(pre, 0.0) - jnp.log(1.0 + jnp.exp(-jnp.abs(pre)))) * (1.0 / GLA_TAU)

    row = lax.broadcasted_iota(jnp.int32, (c, c), 0)
    col = lax.broadcasted_iota(jnp.int32, (c, c), 1)
    tri = (row >= col).astype(BF16)
    gs = _split3(g)
    b = _dot(tri, gs[0]) + _dot(tri, gs[1]) + _dot(tri, gs[2])

    a = jnp.zeros((c, c), F32)
    n = c
    while n > GLA_FLAT:
        half = n // 2
        ref = _rows_bcast(b, n, half)
        qh = (q * jnp.exp(jnp.minimum(b - ref, 0.0))).astype(BF16)
        kh = (k * jnp.exp(jnp.minimum(ref - b, 0.0))).astype(BF16)
        sh = n.bit_length() - 1
        mask = ((row >> sh) == (col >> sh)) & ((row & (n - 1)) >= half) & ((col & (n - 1)) < half)
        a = jnp.where(mask, _dot_nt(qh, kh), a)
        n = half
    ref = _rows_bcast(b, GLA_FLAT, 0)
    qh = (q * jnp.exp(b - ref)).astype(BF16)
    kh = (k * jnp.exp(ref - b)).astype(BF16)
    sh = GLA_FLAT.bit_length() - 1
    mask = ((row >> sh) == (col >> sh)) & (col <= row)
    a = jnp.where(mask, _dot_nt(qh, kh), a)

    st = st_ref[...]
    o = _dot(a.astype(BF16), v) + _dot_nt((q * jnp.exp(b)).astype(BF16), st.astype(BF16))

    b_last = b[c - 1:c, :]
    kd = (k * jnp.exp(b_last - b)).astype(BF16)
    st_ref[...] = st * jnp.exp(b_last) + _dot_tn(v, kd)

    on = o * lax.rsqrt(jnp.mean(o * o, axis=-1, keepdims=True) + EPS) * ng_ref[...]
    r = r_ref[...].astype(F32) + br_ref[...]
    o_ref[...] = (r * _sigmoid(r) * on).astype(o_ref.dtype)


def _gla_scan(proj, a_low, w_a_up, b_a, b_r, norm_g, bsz, seq):
    t, n3 = proj.shape
    dkk = n3 // 6
    dvv = 2 * dkk
    h = GLA_HEADS
    dk, dv = dkk // h, dvv // h
    c = _pick(seq, GLA_CHUNK)
    nc = seq // c
    voff = (2 * dkk) // dv
    roff = (2 * dkk + dvv) // dv
    rowmap = lambda b, hh, n: b * nc + n
    return pl.pallas_call(
        functools.partial(_gla_scan_kernel, scale=float(dk) ** -0.5),
        grid=(bsz, h, nc),
        in_specs=[pl.BlockSpec((c, dk), lambda b, hh, n: (rowmap(b, hh, n), hh)),
                  pl.BlockSpec((c, dk), lambda b, hh, n: (rowmap(b, hh, n), h + hh)),
                  pl.BlockSpec((c, dv), lambda b, hh, n: (rowmap(b, hh, n), voff + hh)),
                  pl.BlockSpec((c, dv), lambda b, hh, n: (rowmap(b, hh, n), roff + hh)),
                  pl.BlockSpec((c, LANES), lambda b, hh, n: (rowmap(b, hh, n), 0)),
                  pl.BlockSpec((GLA_RANK, dk), lambda b, hh, n: (0, hh)),
                  pl.BlockSpec((1, dk), lambda b, hh, n: (0, hh)),
                  pl.BlockSpec((1, dv), lambda b, hh, n: (0, hh)),
                  pl.BlockSpec((1, dv), lambda b, hh, n: (0, 0))],
        out_specs=pl.BlockSpec((c, dv), lambda b, hh, n: (rowmap(b, hh, n), hh)),
        out_shape=jax.ShapeDtypeStruct((t, dvv), BF16),
        scratch_shapes=[pltpu.VMEM((dv, dk), F32)],
        compiler_params=_params("parallel", "parallel", "arbitrary"),
        name="gla_scan",
    )(proj, proj, proj, proj, a_low, w_a_up, b_a.reshape(1, dkk), b_r.reshape(1, dvv),
      norm_g.reshape(1, dv))


def _sgu_spatial_kernel(u_ref, v_ref, lg_ref, lb_ref, ws_ref, bs_ref, o_ref):
    c, half = v_ref.shape
    gw = half // SGU_GROUPS
    v = v_ref[...].astype(F32)
    mu = jnp.mean(v, axis=-1, keepdims=True)
    vc = v - mu
    var = jnp.mean(vc * vc, axis=-1, keepdims=True)
    vn = (vc * lax.rsqrt(var + EPS) * lg_ref[...] + lb_ref[...]).astype(BF16)
    row = lax.broadcasted_iota(jnp.int32, (c, c), 0)
    col = lax.broadcasted_iota(jnp.int32, (c, c), 1)
    causal = row >= col
    for g in range(SGU_GROUPS):
        w = jnp.where(causal, ws_ref[g], 0.0).astype(BF16)
        mixed = _dot(w, vn[:, g * gw:(g + 1) * gw]) + bs_ref[:, g:g + 1]
        u = u_ref[:, g * gw:(g + 1) * gw].astype(F32)
        o_ref[:, g * gw:(g + 1) * gw] = (u * mixed).astype(o_ref.dtype)


def _sgu_spatial(z, ln_g, ln_b, w_s, b_s):
    t, two_half = z.shape
    half = two_half // 2
    c = SGU_CHUNK
    return pl.pallas_call(
        _sgu_spatial_kernel,
        grid=(t // c,),
        in_specs=[pl.BlockSpec((c, half), lambda i: (i, 0)),
                  pl.BlockSpec((c, half), lambda i: (i, 1)),
                  pl.BlockSpec((1, half), lambda i: (0, 0)),
                  pl.BlockSpec((1, half), lambda i: (0, 0)),
                  pl.BlockSpec((SGU_GROUPS, c, c), lambda i: (0, 0, 0)),
                  pl.BlockSpec((c, SGU_GROUPS), lambda i: (0, 0))],
        out_specs=pl.BlockSpec((c, half), lambda i: (i, 0)),
        out_shape=jax.ShapeDtypeStruct((t, half), BF16),
        compiler_params=_params("parallel"),
        name="sgu_spatial",
    )(z, z, ln_g.reshape(1, half), ln_b.reshape(1, half), w_s, b_s.T)


def _router_kernel(x_ref, g_ref, sh_ref, sc_ref, w_ref, b_ref, h_ref, id_ref, gt_ref):
    h = _norm_mod(x_ref[...], g_ref[...], sh_ref[0], sc_ref[0])
    h_ref[...] = h
    hi, mid, _ = _split3(h)
    w = w_ref[...]
    l2 = _dot(hi, w) + _dot(mid, w)
    logits = l2[:, :LANES] + l2[:, LANES:] + b_ref[...]
    lane = lax.broadcasted_iota(jnp.int32, logits.shape, 1)
    lane_f = lane.astype(F32)
    neg = -jnp.inf
    big = float(LANES)

    grp = jnp.where(lane < MOE_GROUPS, logits, neg)
    gmax = jnp.max(grp, axis=-1, keepdims=True)
    gidx = jnp.min(jnp.where(grp == gmax, lane_f, big), axis=-1, keepdims=True)
    g_w = 1.0 / jnp.sum(jnp.exp(grp - gmax), axis=-1, keepdims=True)

    e_lane = lane - MOE_GROUPS
    in_grp = (e_lane >= 0) & (e_lane < MOE_EXPERTS) & ((e_lane // MOE_PER_GROUP).astype(F32) == gidx)
    el = jnp.where(in_grp, logits, neg)
    m1 = jnp.max(el, axis=-1, keepdims=True)
    i1 = jnp.min(jnp.where(el == m1, lane_f, big), axis=-1, keepdims=True)
    el2 = jnp.where(lane_f == i1, neg, el)
    m2 = jnp.max(el2, axis=-1, keepdims=True)
    i2 = jnp.min(jnp.where(el2 == m2, lane_f, big), axis=-1, keepdims=True)
    z = jnp.sum(jnp.exp(el - m1), axis=-1, keepdims=True)
    p1 = 1.0 / z
    p2 = jnp.exp(m2 - m1) / z
    psum = p1 + p2
    ids = jnp.where(lane == 0, i1, jnp.where(lane == 1, i2, float(MOE_GROUPS))) - float(MOE_GROUPS)
    id_ref[...] = ids.astype(jnp.int32)
    gt_ref[...] = jnp.where(lane == 0, g_w * (p1 / psum), jnp.where(lane == 1, g_w * (p2 / psum), 0.0))


def _router(x, gain, shift, scale, w_grp, b_grp, w_exp, b_exp, rows_per_batch, tm=512):
    t, d = x.shape
    bsz = shift.shape[0]
    tm = _pick(rows_per_batch, tm)
    tpb = rows_per_batch // tm
    n_log = MOE_GROUPS + MOE_EXPERTS
    wcat = jnp.zeros((d, LANES), F32).at[:, :MOE_GROUPS].set(w_grp).at[:, MOE_GROUPS:n_log].set(w_exp)
    w_hi = wcat.astype(BF16)
    w_lo = (wcat - w_hi.astype(F32)).astype(BF16)
    w2 = jnp.concatenate([w_hi, w_lo], axis=1)
    bcat = jnp.zeros((1, LANES), F32).at[0, :MOE_GROUPS].set(b_grp).at[0, MOE_GROUPS:n_log].set(b_exp)
    return pl.pallas_call(
        _router_kernel,
        grid=(t // tm,),
        in_specs=[pl.BlockSpec((tm, d), lambda i: (i, 0)),
                  pl.BlockSpec((1, d), lambda i: (0, 0)),
                  pl.BlockSpec((1, 1, d), lambda i: (i // tpb, 0, 0)),
                  pl.BlockSpec((1, 1, d), lambda i: (i // tpb, 0, 0)),
                  pl.BlockSpec((d, 2 * LANES), lambda i: (0, 0)),
                  pl.BlockSpec((1, LANES), lambda i: (0, 0))],
        out_specs=[pl.BlockSpec((tm, d), lambda i: (i, 0)),
                   pl.BlockSpec((tm, LANES), lambda i: (i, 0)),
                   pl.BlockSpec((tm, LANES), lambda i: (i, 0))],
        out_shape=[jax.ShapeDtypeStruct((t, d), F32),
                   jax.ShapeDtypeStruct((t, LANES), jnp.int32),
                   jax.ShapeDtypeStruct((t, LANES), F32)],
        compiler_params=_params("parallel"),
        name="moe_router",
    )(x, gain.reshape(1, d), shift.reshape(bsz, 1, d), scale.reshape(bsz, 1, d), w2, bcat)


def _experts_kernel(bexp_ref, tokc_ref, tokn_ref, dst_ref, wt_ref, h_hbm, w1_ref, w3_ref, w2_ref,
                    y_hbm, xbuf, ybuf, w1b, w3b, w2b, gsem, ssem, *, n_assign):
    blk = xbuf.shape[1]
    i = pl.program_id(0)
    nb = pl.num_programs(0)
    slot = lax.rem(i, 2)

    def gather(tok, r, s):
        return pltpu.make_async_copy(h_hbm.at[pl.ds(tok, 1), :], xbuf.at[s, pl.ds(r, 1), :], gsem.at[s])

    def scatter(dst, r, s):
        return pltpu.make_async_copy(ybuf.at[s, pl.ds(r, 1), :], y_hbm.at[pl.ds(dst, 1), :], ssem.at[s])

    def dump_fill(s):
        return pltpu.make_async_copy(ybuf.at[s], y_hbm.at[pl.ds(n_assign + s * blk, blk), :], ssem.at[s])

    @pl.when(i == 0)
    def _():
        ybuf[...] = jnp.zeros_like(ybuf)
        for s in range(2):
            dump_fill(s).start()
        for s in range(2):
            dump_fill(s).wait()
        for r in range(blk):
            gather(tokc_ref[0, 0, r], r, 0).start()

    for r in range(blk):
        gather(0, r, slot).wait()

    @pl.when(i + 1 < nb)
    def _():
        for r in range(blk):
            gather(tokn_ref[0, 0, r], r, 1 - slot).start()

    e = bexp_ref[i]
    e_prev = bexp_ref[jnp.maximum(i - 1, 0)]

    @pl.when((i == 0) | (e != e_prev))
    def _():
        w1b[...] = w1_ref[0].astype(BF16)
        w3b[...] = w3_ref[0].astype(BF16)
        w2b[...] = w2_ref[0].astype(BF16)

    x = xbuf[slot].astype(BF16)
    a = _dot(x, w1b[...])
    hb = (a * _sigmoid(a)) * _dot(x, w3b[...])
    y = _dot(hb.astype(BF16), w2b[...]) * wt_ref[0]

    @pl.when(i >= 2)
    def _():
        for r in range(blk):
            scatter(0, r, slot).wait()

    ybuf[slot] = y
    for r in range(blk):
        scatter(dst_ref[0, 0, r], r, slot).start()

    @pl.when(i == nb - 1)
    def _():
        for r in range(blk):
            scatter(0, r, slot).wait()
        for r in range(blk):
            scatter(0, r, 1 - slot).wait()


def _experts(h2, block_exp, buf_tok, buf_dst, buf_w, w1, w3, w2, n_assign):
    t, d = h2.shape
    e, _, hid = w1.shape
    blk = MOE_BLOCK
    nb = block_exp.shape[0]
    assert nb >= 2
    tok3 = buf_tok.reshape(nb, 1, blk)
    dst3 = buf_dst.reshape(nb, 1, blk)
    w3d = buf_w.reshape(nb, blk, 1)
    smem = functools.partial(pl.BlockSpec, memory_space=pltpu.SMEM)
    grid_spec = pltpu.PrefetchScalarGridSpec(
        num_scalar_prefetch=1,
        grid=(nb,),
        in_specs=[smem((1, 1, blk), lambda i, be: (i, 0, 0)),
                  smem((1, 1, blk), lambda i, be: (jnp.minimum(i + 1, nb - 1), 0, 0)),
                  smem((1, 1, blk), lambda i, be: (i, 0, 0)),
                  pl.BlockSpec((1, blk, 1), lambda i, be: (i, 0, 0)),
                  pl.BlockSpec(memory_space=pl.ANY),
                  pl.BlockSpec((1, d, hid), lambda i, be: (be[i], 0, 0)),
                  pl.BlockSpec((1, d, hid), lambda i, be: (be[i], 0, 0)),
                  pl.BlockSpec((1, hid, d), lambda i, be: (be[i], 0, 0))],
        out_specs=pl.BlockSpec(memory_space=pl.ANY),
        scratch_shapes=[pltpu.VMEM((2, blk, d), F32),
                        pltpu.VMEM((2, blk, d), F32),
                        pltpu.VMEM((d, hid), BF16),
                        pltpu.VMEM((d, hid), BF16),
                        pltpu.VMEM((hid, d), BF16),
                        pltpu.SemaphoreType.DMA((2,)),
                        pltpu.SemaphoreType.DMA((2,))],
    )
    return pl.pallas_call(
        functools.partial(_experts_kernel, n_assign=n_assign),
        grid_spec=grid_spec,
        out_shape=jax.ShapeDtypeStruct((n_assign + 2 * blk, d), F32),
        compiler_params=_params("arbitrary"),
        name="moe_experts",
    )(block_exp, tok3, tok3, dst3, w3d, h2, w1, w3, w2)


def _dispatch_tables(ids, gates, t):
    blk = MOE_BLOCK
    a = t * MOE_TOPK
    e_flat = ids.reshape(a)
    w_flat = gates.reshape(a)
    order = jnp.argsort(e_flat).astype(jnp.int32)
    e_sorted = e_flat[order]
    counts = jnp.zeros((MOE_EXPERTS,), jnp.int32).at[e_flat].add(1)
    padded = (counts + blk - 1) // blk * blk
    starts = jnp.cumsum(counts) - counts
    pad_starts = jnp.cumsum(padded) - padded
    pad_ends = pad_starts + padded
    dest = pad_starts[e_sorted] + (jnp.arange(a, dtype=jnp.int32) - starts[e_sorted])
    nb = -(-(a + MOE_EXPERTS * (blk - 1)) // blk)
    p = nb * blk
    buf_asg = jnp.full((p,), -1, jnp.int32).at[dest].set(order)
    buf_w = jnp.zeros((p,), F32).at[dest].set(w_flat[order])
    pos = jnp.arange(p, dtype=jnp.int32)
    valid = buf_asg >= 0
    buf_tok = jnp.where(valid, buf_asg // MOE_TOPK, 0)
    buf_dst = jnp.where(valid, buf_asg, a + ((pos // blk) % 2) * blk + pos % blk)
    block_exp = jnp.minimum(
        jnp.searchsorted(pad_ends, jnp.arange(nb, dtype=jnp.int32) * blk, side='right'),
        MOE_EXPERTS - 1).astype(jnp.int32)
    return block_exp, buf_tok, buf_dst, buf_w


def _combine_kernel(x_ref, y_ref, gate_ref, fn_ref, o_ref, *, final):
    d = x_ref.shape[1]
    y = y_ref[:, :d] + y_ref[:, d:]
    x = x_ref[...] + gate_ref[0] * y
    if final:
        x = x * lax.rsqrt(jnp.mean(x * x, axis=-1, keepdims=True) + EPS) * fn_ref[...]
    o_ref[...] = x


def _combine(x, y2, gate, final_gain, rows_per_batch, final, tm=512):
    t, d = x.shape
    bsz = gate.shape[0]
    tm = _pick(rows_per_batch, tm)
    tpb = rows_per_batch // tm
    y_pairs = y2.reshape(y2.shape[0] // MOE_TOPK, MOE_TOPK * d)
    return pl.pallas_call(
        functools.partial(_combine_kernel, final=final),
        grid=(t // tm,),
        in_specs=[pl.BlockSpec((tm, d), lambda i: (i, 0)),
                  pl.BlockSpec((tm, MOE_TOPK * d), lambda i: (i, 0)),
                  pl.BlockSpec((1, 1, d), lambda i: (i // tpb, 0, 0)),
                  pl.BlockSpec((1, d), lambda i: (0, 0))],
        out_specs=pl.BlockSpec((tm, d), lambda i: (i, 0)),
        out_shape=jax.ShapeDtypeStruct((t, d), F32),
        compiler_params=_params("parallel"),
        name="moe_combine",
    )(x, y_pairs, gate.reshape(bsz, 1, d), final_gain.reshape(1, d))


def _moe(x, gain, shift, scale, gate, w_grp, b_grp, w_exp, b_exp, w1, w3, w2, final_gain, seq, final):
    t, _ = x.shape
    h2, ids, gts = _router(x, gain, shift, scale, w_grp, b_grp, w_exp, b_exp, seq)
    tables = _dispatch_tables(ids[:, :MOE_TOPK], gts[:, :MOE_TOPK], t)
    y2 = _experts(h2, *tables, w1, w3, w2, t * MOE_TOPK)
    return _combine(x, y2, gate, final_gain, seq, final)


def kernel(x, c, norm_mix, norm_ffn, ada_w, ada_b, gla_w_in, gla_b_r, gla_w_a_up, gla_b_a, gla_norm_g, gla_w_out, sgu_w_in, sgu_b_in, sgu_ln_g, sgu_ln_b, sgu_w_s, sgu_b_s, sgu_w_out, moe_w_grp, moe_b_grp, moe_w_exp, moe_b_exp, moe_w1, moe_w3, moe_w2, final_norm):
    bsz, seq, d = x.shape
    depth = norm_mix.shape[0]
    xt = x.reshape(bsz * seq, d)
    mod = _adaln(c, ada_w, ada_b)
    for i in range(depth):
        sh1, sc1, g1, sh2, sc2, g2 = [mod[i, :, m * d:(m + 1) * d] for m in range(N_MOD)]
        j = i // 2
        if i % 2 == 0:
            w_in = gla_w_in[j]
            n_main = w_in.shape[1] - GLA_RANK
            w_main = w_in[:, :n_main].astype(BF16)
            w_low = jnp.zeros((d, LANES), F32).at[:, :GLA_RANK].set(w_in[:, n_main:]).astype(BF16)
            proj = _norm_matmul(xt, norm_mix[i], sh1, sc1, w_main, None, BF16, seq, "gla_in_proj")
            a_low = _norm_matmul(xt, norm_mix[i], sh1, sc1, w_low, None, F32, seq, "gla_low_proj")
            o = _gla_scan(proj, a_low, gla_w_a_up[j], gla_b_a[j], gla_b_r[j], gla_norm_g[j], bsz, seq)
            xt = _matmul_residual(o, gla_w_out[j].astype(BF16), xt, g1, seq, "gla_out_proj")
        else:
            z = _norm_matmul(xt, norm_mix[i], sh1, sc1, sgu_w_in[j].astype(BF16), sgu_b_in[j], BF16,
                             seq, "sgu_in_proj")
            o = _sgu_spatial(z, sgu_ln_g[j], sgu_ln_b[j], sgu_w_s[j], sgu_b_s[j])
            xt = _matmul_residual(o, sgu_w_out[j].astype(BF16), xt, g1, seq, "sgu_out_proj", tm=512)
        xt = _moe(xt, norm_ffn[i], sh2, sc2, g2, moe_w_grp[i], moe_b_grp[i], moe_w_exp[i], moe_b_exp[i],
                  moe_w1[i], moe_w3[i], moe_w2[i], final_norm, seq, final=(i == depth - 1))
    return xt.reshape(bsz, seq, d)
```

```python
import functools
import math

import jax
import jax.numpy as jnp
from jax import lax
from jax.experimental import pallas as pl
from jax.experimental.pallas import tpu as pltpu

F32 = jnp.float32
BF16 = jnp.bfloat16

EPS = 1e-6
N_MOD = 6
GLA_HEADS = 4
GLA_RANK = 16
GLA_TAU = 16.0
GLA_CHUNK = 128
GLA_FLAT = 8
SGU_CHUNK = 128
SGU_GROUPS = 8
MOE_GROUPS = 8
MOE_PER_GROUP = 8
MOE_EXPERTS = MOE_GROUPS * MOE_PER_GROUP
MOE_TOPK = 2
MOE_BLOCK = 128
MOE_TOKEN_TILE = 256
LANES = 128
V7X_VMEM_LIMIT_BYTES = 56 * 2**20


def _params(*semantics):
    return pltpu.CompilerParams(dimension_semantics=semantics,
                                vmem_limit_bytes=V7X_VMEM_LIMIT_BYTES)


def _dot(a, b):
    return jnp.dot(a, b, preferred_element_type=F32)


def _dot_nt(a, b):
    return lax.dot_general(a, b, (((1,), (1,)), ((), ())), preferred_element_type=F32)


def _dot_tn(a, b):
    return lax.dot_general(a, b, (((0,), (0,)), ((), ())), preferred_element_type=F32)


def _split3(x):
    hi = x.astype(BF16)
    r1 = x - hi.astype(F32)
    mid = r1.astype(BF16)
    lo = (r1 - mid.astype(F32)).astype(BF16)
    return hi, mid, lo


def _sigmoid(x):
    return 1.0 / (1.0 + jnp.exp(-x))


def _gelu_tanh(x):
    c = math.sqrt(2.0 / math.pi)
    return x * (0.5 * (1.0 + jnp.tanh(c * (x + 0.044715 * (x * x * x)))))


def _pick(n, pref):
    t = min(n, pref)
    while n % t:
        t //= 2
    return t


def _adaln_kernel(c_ref, w_ref, b_ref, o_ref):
    c = c_ref[...]
    ca = c * _sigmoid(c)
    hi, mid, _ = _split3(ca)
    w = w_ref[0].astype(BF16)
    o_ref[0] = _dot(hi, w) + _dot(mid, w) + b_ref[0]


def _adaln(c, ada_w, ada_b):
    depth, d, n = ada_w.shape
    b = c.shape[0]
    rows = 8
    cp = jnp.zeros((rows, d), F32).at[:b].set(c)
    tn = _pick(n, 1024)
    out = pl.pallas_call(
        _adaln_kernel,
        grid=(depth, n // tn),
        in_specs=[pl.BlockSpec((rows, d), lambda l, j: (0, 0)),
                  pl.BlockSpec((1, d, tn), lambda l, j: (l, 0, j)),
                  pl.BlockSpec((1, 1, tn), lambda l, j: (l, 0, j))],
        out_specs=pl.BlockSpec((1, rows, tn), lambda l, j: (l, 0, j)),
        out_shape=jax.ShapeDtypeStruct((depth, rows, n), F32),
        compiler_params=_params("parallel", "parallel"),
        name="adaln",
    )(cp, ada_w, ada_b.reshape(depth, 1, n))
    return out[:, :b]


def _norm_mod(x, gain, shift, scale):
    ms = jnp.mean(x * x, axis=-1, keepdims=True)
    return (x * lax.rsqrt(ms + EPS) * gain) * (1.0 + scale) + shift


def _norm_matmul_kernel(x_ref, g_ref, sh_ref, sc_ref, w_ref, *rest, gelu):
    if gelu:
        bias_ref, o_ref, h_scr = rest
    else:
        o_ref, h_scr = rest

    @pl.when(pl.program_id(1) == 0)
    def _():
        h_scr[...] = _norm_mod(x_ref[...], g_ref[...], sh_ref[0], sc_ref[0]).astype(BF16)

    acc = _dot(h_scr[...], w_ref[...])
    if gelu:
        acc = _gelu_tanh(acc + bias_ref[...])
    o_ref[...] = acc.astype(o_ref.dtype)


def _norm_matmul(x, gain, shift, scale, w, bias, out_dtype, rows_per_batch, name, tm=1024, tn=1024):
    t, d = x.shape
    n = w.shape[1]
    tm = _pick(rows_per_batch, tm)
    tn = _pick(n, tn)
    tpb = rows_per_batch // tm
    bsz = shift.shape[0]
    in_specs = [pl.BlockSpec((tm, d), lambda i, j: (i, 0)),
                pl.BlockSpec((1, d), lambda i, j: (0, 0)),
                pl.BlockSpec((1, 1, d), lambda i, j: (i // tpb, 0, 0)),
                pl.BlockSpec((1, 1, d), lambda i, j: (i // tpb, 0, 0)),
                pl.BlockSpec((d, tn), lambda i, j: (0, j))]
    args = [x, gain.reshape(1, d), shift.reshape(bsz, 1, d), scale.reshape(bsz, 1, d), w]
    if bias is not None:
        in_specs.append(pl.BlockSpec((1, tn), lambda i, j: (0, j)))
        args.append(bias.reshape(1, n))
    return pl.pallas_call(
        functools.partial(_norm_matmul_kernel, gelu=bias is not None),
        grid=(t // tm, n // tn),
        in_specs=in_specs,
        out_specs=pl.BlockSpec((tm, tn), lambda i, j: (i, j)),
        out_shape=jax.ShapeDtypeStruct((t, n), out_dtype),
        scratch_shapes=[pltpu.VMEM((tm, d), BF16)],
        compiler_params=_params("parallel", "arbitrary"),
        name=name,
    )(*args)


def _matmul_residual_kernel(a_ref, w_ref, x_ref, gate_ref, o_ref):
    o_ref[...] = x_ref[...] + gate_ref[0] * _dot(a_ref[...], w_ref[...])


def _matmul_residual(a, w, x, gate, rows_per_batch, name, tm=1024, tn=1024):
    t, k = a.shape
    n = w.shape[1]
    tm = _pick(rows_per_batch, tm)
    tn = _pick(n, tn)
    tpb = rows_per_batch // tm
    bsz = gate.shape[0]
    return pl.pallas_call(
        _matmul_residual_kernel,
        grid=(t // tm, n // tn),
        in_specs=[pl.BlockSpec((tm, k), lambda i, j: (i, 0)),
                  pl.BlockSpec((k, tn), lambda i, j: (0, j)),
                  pl.BlockSpec((tm, tn), lambda i, j: (i, j)),
                  pl.BlockSpec((1, 1, tn), lambda i, j: (i // tpb, 0, j))],
        out_specs=pl.BlockSpec((tm, tn), lambda i, j: (i, j)),
        out_shape=jax.ShapeDtypeStruct((t, n), F32),
        compiler_params=_params("parallel", "parallel"),
        name=name,
    )(a, w, x, gate.reshape(bsz, 1, n))


def _rows_bcast(b, n, off):
    c, dk = b.shape
    parts = [jnp.broadcast_to(b[j * n + off:j * n + off + 1, :], (n, dk)) for j in range(c // n)]
    return parts[0] if len(parts) == 1 else jnp.concatenate(parts, axis=0)


def _gla_scan_kernel(q_ref, k_ref, v_ref, r_ref, al_ref, wup_ref, ba_ref, br_ref, ng_ref,
                     o_ref, st_ref, *, scale):
    c, dk = q_ref.shape

    @pl.when(pl.program_id(2) == 0)
    def _():
        st_ref[...] = jnp.zeros_like(st_ref)

    q = q_ref[...].astype(F32) * scale
    k = k_ref[...].astype(F32)
    v = v_ref[...]

    al = _split3(al_ref[:, :GLA_RANK])
    wu = _split3(wup_ref[...])
    pre = (_dot(al[0], wu[0]) + _dot(al[0], wu[1]) + _dot(al[1], wu[0])
           + _dot(al[1], wu[1]) + _dot(al[0], wu[2]) + _dot(al[2], wu[0])) + ba_ref[...]
    g = (jnp.minimum(pre, 0.0) - jnp.log(1.0 + jnp.exp(-jnp.abs(pre)))) * (1.0 / GLA_TAU)

    row = lax.broadcasted_iota(jnp.int32, (c, c), 0)
    col = lax.broadcasted_iota(jnp.int32, (c, c), 1)
    tri = (row >= col).astype(BF16)
    gs = _split3(g)
    b = _dot(tri, gs[0]) + _dot(tri, gs[1]) + _dot(tri, gs[2])

    a = jnp.zeros((c, c), F32)
    n = c
    while n > GLA_FLAT:
        half = n // 2
        ref = _rows_bcast(b, n, half)
        qh = (q * jnp.exp(jnp.minimum(b - ref, 0.0))).astype(BF16)
        kh = (k * jnp.exp(jnp.minimum(ref - b, 0.0))).astype(BF16)
        sh = n.bit_length() - 1
        mask = ((row >> sh) == (col >> sh)) & ((row & (n - 1)) >= half) & ((col & (n - 1)) < half)
        a = jnp.where(mask, _dot_nt(qh, kh), a)
        n = half
    ref = _rows_bcast(b, GLA_FLAT, 0)
    qh = (q * jnp.exp(b - ref)).astype(BF16)
    kh = (k * jnp.exp(ref - b)).astype(BF16)
    sh = GLA_FLAT.bit_length() - 1
    mask = ((row >> sh) == (col >> sh)) & (col <= row)
    a = jnp.where(mask, _dot_nt(qh, kh), a)

    st = st_ref[...]
    o = _dot(a.astype(BF16), v) + _dot_nt((q * jnp.exp(b)).astype(BF16), st.astype(BF16))

    b_last = b[c - 1:c, :]
    kd = (k * jnp.exp(b_last - b)).astype(BF16)
    st_ref[...] = st * jnp.exp(b_last) + _dot_tn(v, kd)

    on = o * lax.rsqrt(jnp.mean(o * o, axis=-1, keepdims=True) + EPS) * ng_ref[...]
    r = r_ref[...].astype(F32) + br_ref[...]
    o_ref[...] = (r * _sigmoid(r) * on).astype(o_ref.dtype)


def _gla_scan(proj, a_low, w_a_up, b_a, b_r, norm_g, bsz, seq):
    t, n3 = proj.shape
    dkk = n3 // 6
    dvv = 2 * dkk
    h = GLA_HEADS
    dk, dv = dkk // h, dvv // h
    c = _pick(seq, GLA_CHUNK)
    nc = seq // c
    voff = (2 * dkk) // dv
    roff = (2 * dkk + dvv) // dv
    rowmap = lambda b, hh, n: b * nc + n
    return pl.pallas_call(
        functools.partial(_gla_scan_kernel, scale=float(dk) ** -0.5),
        grid=(bsz, h, nc),
        in_specs=[pl.BlockSpec((c, dk), lambda b, hh, n: (rowmap(b, hh, n), hh)),
                  pl.BlockSpec((c, dk), lambda b, hh, n: (rowmap(b, hh, n), h + hh)),
                  pl.BlockSpec((c, dv), lambda b, hh, n: (rowmap(b, hh, n), voff + hh)),
                  pl.BlockSpec((c, dv), lambda b, hh, n: (rowmap(b, hh, n), roff + hh)),
                  pl.BlockSpec((c, LANES), lambda b, hh, n: (rowmap(b, hh, n), 0)),
                  pl.BlockSpec((GLA_RANK, dk), lambda b, hh, n: (0, hh)),
                  pl.BlockSpec((1, dk), lambda b, hh, n: (0, hh)),
                  pl.BlockSpec((1, dv), lambda b, hh, n: (0, hh)),
                  pl.BlockSpec((1, dv), lambda b, hh, n: (0, 0))],
        out_specs=pl.BlockSpec((c, dv), lambda b, hh, n: (rowmap(b, hh, n), hh)),
        out_shape=jax.ShapeDtypeStruct((t, dvv), BF16),
        scratch_shapes=[pltpu.VMEM((dv, dk), F32)],
        compiler_params=_params("parallel", "parallel", "arbitrary"),
        name="gla_scan",
    )(proj, proj, proj, proj, a_low, w_a_up, b_a.reshape(1, dkk), b_r.reshape(1, dvv),
      norm_g.reshape(1, dv))


def _sgu_spatial_kernel(u_ref, v_ref, lg_ref, lb_ref, ws_ref, bs_ref, o_ref):
    c, half = v_ref.shape
    gw = half // SGU_GROUPS
    v = v_ref[...].astype(F32)
    mu = jnp.mean(v, axis=-1, keepdims=True)
    vc = v - mu
    var = jnp.mean(vc * vc, axis=-1, keepdims=True)
    vn = (vc * lax.rsqrt(var + EPS) * lg_ref[...] + lb_ref[...]).astype(BF16)
    row = lax.broadcasted_iota(jnp.int32, (c, c), 0)
    col = lax.broadcasted_iota(jnp.int32, (c, c), 1)
    causal = row >= col
    for g in range(SGU_GROUPS):
        w = jnp.where(causal, ws_ref[g], 0.0).astype(BF16)
        mixed = _dot(w, vn[:, g * gw:(g + 1) * gw]) + bs_ref[:, g:g + 1]
        u = u_ref[:, g * gw:(g + 1) * gw].astype(F32)
        o_ref[:, g * gw:(g + 1) * gw] = (u * mixed).astype(o_ref.dtype)


def _sgu_spatial(z, ln_g, ln_b, w_s, b_s):
    t, two_half = z.shape
    half = two_half // 2
    c = SGU_CHUNK
    return pl.pallas_call(
        _sgu_spatial_kernel,
        grid=(t // c,),
        in_specs=[pl.BlockSpec((c, half), lambda i: (i, 0)),
                  pl.BlockSpec((c, half), lambda i: (i, 1)),
                  pl.BlockSpec((1, half), lambda i: (0, 0)),
                  pl.BlockSpec((1, half), lambda i: (0, 0)),
                  pl.BlockSpec((SGU_GROUPS, c, c), lambda i: (0, 0, 0)),
                  pl.BlockSpec((c, SGU_GROUPS), lambda i: (0, 0))],
        out_specs=pl.BlockSpec((c, half), lambda i: (i, 0)),
        out_shape=jax.ShapeDtypeStruct((t, half), BF16),
        compiler_params=_params("parallel"),
        name="sgu_spatial",
    )(z, z, ln_g.reshape(1, half), ln_b.reshape(1, half), w_s, b_s.T)


def _router_kernel(x_ref, g_ref, sh_ref, sc_ref, w_ref, b_ref, h_ref, id_ref, gt_ref, cnt_ref, cnt_scr):
    @pl.when(pl.program_id(0) == 0)
    def _():
        cnt_scr[...] = jnp.zeros_like(cnt_scr)

    h = _norm_mod(x_ref[...], g_ref[...], sh_ref[0], sc_ref[0])
    h_ref[...] = h
    hi, mid, _ = _split3(h)
    w = w_ref[...]
    l2 = _dot(hi, w) + _dot(mid, w)
    logits = l2[:, :LANES] + l2[:, LANES:] + b_ref[...]
    lane = lax.broadcasted_iota(jnp.int32, logits.shape, 1)
    lane_f = lane.astype(F32)
    neg = -jnp.inf
    big = float(LANES)

    grp = jnp.where(lane < MOE_GROUPS, logits, neg)
    gmax = jnp.max(grp, axis=-1, keepdims=True)
    gidx = jnp.min(jnp.where(grp == gmax, lane_f, big), axis=-1, keepdims=True)
    g_w = 1.0 / jnp.sum(jnp.exp(grp - gmax), axis=-1, keepdims=True)

    e_lane = lane - MOE_GROUPS
    in_grp = (e_lane >= 0) & (e_lane < MOE_EXPERTS) & ((e_lane // MOE_PER_GROUP).astype(F32) == gidx)
    el = jnp.where(in_grp, logits, neg)
    m1 = jnp.max(el, axis=-1, keepdims=True)
    i1 = jnp.min(jnp.where(el == m1, lane_f, big), axis=-1, keepdims=True)
    el2 = jnp.where(lane_f == i1, neg, el)
    m2 = jnp.max(el2, axis=-1, keepdims=True)
    i2 = jnp.min(jnp.where(el2 == m2, lane_f, big), axis=-1, keepdims=True)
    z = jnp.sum(jnp.exp(el - m1), axis=-1, keepdims=True)
    p1 = 1.0 / z
    p2 = jnp.exp(m2 - m1) / z
    psum = p1 + p2
    e1 = i1 - float(MOE_GROUPS)
    e2 = i2 - float(MOE_GROUPS)

    tm = logits.shape[0]
    oh1 = (lane_f == e1).astype(F32)
    oh2 = (lane_f == e2).astype(F32)
    oh = oh1 + oh2
    row = lax.broadcasted_iota(jnp.int32, (tm, tm), 0)
    col = lax.broadcasted_iota(jnp.int32, (tm, tm), 1)
    before = _dot((row > col).astype(BF16), oh.astype(BF16)) + cnt_scr[...]
    r1 = jnp.sum(oh1 * before, axis=-1, keepdims=True)
    r2 = jnp.sum(oh2 * before, axis=-1, keepdims=True)
    cnt = cnt_scr[...] + jnp.sum(oh, axis=0, keepdims=True)
    cnt_scr[...] = cnt
    cnt_ref[...] = jnp.broadcast_to(cnt, cnt_ref.shape).astype(jnp.int32)

    meta = jnp.where(lane == 0, e1, jnp.where(lane == 1, e2, jnp.where(lane == 2, r1, jnp.where(lane == 3, r2, 0.0))))
    id_ref[...] = meta.astype(jnp.int32)
    gt_ref[...] = jnp.where(lane == 0, g_w * (p1 / psum), jnp.where(lane == 1, g_w * (p2 / psum), 0.0))


def _router(x, gain, shift, scale, w_grp, b_grp, w_exp, b_exp, rows_per_batch, tm=512):
    t, d = x.shape
    bsz = shift.shape[0]
    tm = _pick(rows_per_batch, tm)
    tpb = rows_per_batch // tm
    n_log = MOE_GROUPS + MOE_EXPERTS
    wcat = jnp.zeros((d, LANES), F32).at[:, :MOE_GROUPS].set(w_grp).at[:, MOE_GROUPS:n_log].set(w_exp)
    w_hi = wcat.astype(BF16)
    w_lo = (wcat - w_hi.astype(F32)).astype(BF16)
    w2 = jnp.concatenate([w_hi, w_lo], axis=1)
    bcat = jnp.zeros((1, LANES), F32).at[0, :MOE_GROUPS].set(b_grp).at[0, MOE_GROUPS:n_log].set(b_exp)
    return pl.pallas_call(
        _router_kernel,
        grid=(t // tm,),
        in_specs=[pl.BlockSpec((tm, d), lambda i: (i, 0)),
                  pl.BlockSpec((1, d), lambda i: (0, 0)),
                  pl.BlockSpec((1, 1, d), lambda i: (i // tpb, 0, 0)),
                  pl.BlockSpec((1, 1, d), lambda i: (i // tpb, 0, 0)),
                  pl.BlockSpec((d, 2 * LANES), lambda i: (0, 0)),
                  pl.BlockSpec((1, LANES), lambda i: (0, 0))],
        out_specs=[pl.BlockSpec((tm, d), lambda i: (i, 0)),
                   pl.BlockSpec((tm, LANES), lambda i: (i, 0)),
                   pl.BlockSpec((tm, LANES), lambda i: (i, 0)),
                   pl.BlockSpec((8, LANES), lambda i: (0, 0))],
        out_shape=[jax.ShapeDtypeStruct((t, d), F32),
                   jax.ShapeDtypeStruct((t, LANES), jnp.int32),
                   jax.ShapeDtypeStruct((t, LANES), F32),
                   jax.ShapeDtypeStruct((8, LANES), jnp.int32)],
        scratch_shapes=[pltpu.VMEM((1, LANES), F32)],
        compiler_params=_params("arbitrary"),
        name="moe_router",
    )(x, gain.reshape(1, d), shift.reshape(bsz, 1, d), scale.reshape(bsz, 1, d), w2, bcat)


def _dispatch_tables(ids, rank, counts, n_tiles, tile):
    blk = MOE_BLOCK
    a = ids.shape[0] * MOE_TOPK
    padded = (counts + blk - 1) // blk * blk
    pad_ends = jnp.cumsum(padded)
    pad_starts = pad_ends - padded
    onehot = ids[:, :, None] == jnp.arange(MOE_EXPERTS, dtype=jnp.int32)
    dest = jnp.sum(jnp.where(onehot, pad_starts, 0), axis=-1) + rank
    dest_tiles = dest.reshape(n_tiles, tile, MOE_TOPK).transpose(0, 2, 1).reshape(n_tiles, 1, MOE_TOPK * tile)
    nb = -(-(a + MOE_EXPERTS * (blk - 1)) // blk)
    block_start = jnp.arange(nb, dtype=jnp.int32) * blk
    block_exp = jnp.minimum(jnp.sum(pad_ends[None, :] <= block_start[:, None], axis=1), MOE_EXPERTS - 1)
    n_used = (pad_ends[-1] // blk).reshape(1)
    fill = jnp.stack([pad_starts + counts, padded - counts,
                      jnp.broadcast_to(n_used, (MOE_EXPERTS,))])
    return dest_tiles.astype(jnp.int32), block_exp.astype(jnp.int32), n_used.astype(jnp.int32), fill.astype(jnp.int32)


SUBLANES = 8


def _dispatch_kernel(fill_ref, dest_ref, h_ref, xs_hbm, zbuf, sem, zsem):
    tm = h_ref.shape[0]
    i = pl.program_id(0)

    def fill_copies(start):
        def issue(cp, on):
            @pl.when(on)
            def _():
                cp.start() if start else cp.wait()

        for e in range(MOE_EXPERTS):
            off = fill_ref[0, e]
            n_pad = fill_ref[1, e]
            head = n_pad & (SUBLANES - 1)
            for hrow in range(SUBLANES - 1):
                issue(pltpu.make_async_copy(zbuf.at[pl.ds(0, 1), :], xs_hbm.at[pl.ds(off + hrow, 1), :], zsem),
                      hrow < head)
            off = off + head
            bit = MOE_BLOCK // 2
            while bit >= SUBLANES:
                issue(pltpu.make_async_copy(zbuf.at[pl.ds(0, bit), :],
                                            xs_hbm.at[pl.ds(pl.multiple_of(off, SUBLANES), bit), :], zsem),
                      (n_pad & bit) != 0)
                off = off + (n_pad & bit)
                bit //= 2

        half = zbuf.shape[0]
        first = fill_ref[2, 0] * (MOE_BLOCK // half)

        def tail(j, c):
            cp = pltpu.make_async_copy(zbuf, xs_hbm.at[pl.ds(pl.multiple_of(j * half, SUBLANES), half), :], zsem)
            cp.start() if start else cp.wait()
            return c
        lax.fori_loop(first, xs_hbm.shape[0] // half, tail, 0)

    @pl.when(i == 0)
    def _():
        zbuf[...] = jnp.zeros_like(zbuf)
        fill_copies(True)
        fill_copies(False)

    def row_copy(j):
        r = lax.rem(j, tm)
        return pltpu.make_async_copy(h_ref.at[pl.ds(r, 1), :], xs_hbm.at[pl.ds(dest_ref[0, 0, j], 1), :], sem)

    def start(j, c):
        row_copy(j).start()
        return c

    def wait(j, c):
        row_copy(j).wait()
        return c

    lax.fori_loop(0, MOE_TOPK * tm, start, 0, unroll=8)
    lax.fori_loop(0, MOE_TOPK * tm, wait, 0, unroll=8)


def _dispatch(h2, dest_tiles, fill, n_rows):
    t, d = h2.shape
    nt, _, two_tm = dest_tiles.shape
    tm = two_tm // MOE_TOPK
    grid_spec = pltpu.PrefetchScalarGridSpec(
        num_scalar_prefetch=1,
        grid=(nt,),
        in_specs=[pl.BlockSpec((1, 1, two_tm), lambda i, f: (i, 0, 0), memory_space=pltpu.SMEM),
                  pl.BlockSpec((tm, d), lambda i, f: (i, 0))],
        out_specs=pl.BlockSpec(memory_space=pl.ANY),
        scratch_shapes=[pltpu.VMEM((MOE_BLOCK // 2, d), F32),
                        pltpu.SemaphoreType.DMA(()),
                        pltpu.SemaphoreType.DMA(())],
    )
    return pl.pallas_call(
        _dispatch_kernel,
        grid_spec=grid_spec,
        out_shape=jax.ShapeDtypeStruct((n_rows, d), F32),
        compiler_params=_params("arbitrary"),
        name="moe_dispatch",
    )(fill, dest_tiles, h2)


def _experts_kernel(bexp_ref, nused_ref, x_ref, w1_ref, w3_ref, w2_ref, y_ref, w1b, w3b, w2b):
    i = pl.program_id(0)
    used = i < nused_ref[0]
    e = bexp_ref[i]
    e_prev = bexp_ref[jnp.maximum(i - 1, 0)]

    @pl.when(used & ((i == 0) | (e != e_prev)))
    def _():
        w1b[...] = w1_ref[0, 0].astype(BF16)
        w3b[...] = w3_ref[0, 0].astype(BF16)
        w2b[...] = w2_ref[0, 0].astype(BF16)

    @pl.when(used)
    def _():
        x = x_ref[...].astype(BF16)
        a = _dot(x, w1b[...])
        hb = (a * _sigmoid(a)) * _dot(x, w3b[...])
        y_ref[...] = _dot(hb.astype(BF16), w2b[...])

    @pl.when(jnp.logical_not(used))
    def _():
        y_ref[...] = jnp.zeros_like(y_ref)


def _experts(xs, block_exp, n_used, w1, w3, w2, layer):
    nb = block_exp.shape[0]
    d = xs.shape[1]
    hid = w1.shape[-1]
    blk = MOE_BLOCK

    def xmap(i, be, nu):
        return (jnp.minimum(i, jnp.maximum(nu[0], 1) - 1), 0)

    def wmap(i, be, nu):
        return (layer, be[i], 0, 0)

    grid_spec = pltpu.PrefetchScalarGridSpec(
        num_scalar_prefetch=2,
        grid=(nb,),
        in_specs=[pl.BlockSpec((blk, d), xmap),
                  pl.BlockSpec((1, 1, d, hid), wmap),
                  pl.BlockSpec((1, 1, d, hid), wmap),
                  pl.BlockSpec((1, 1, hid, d), wmap)],
        out_specs=pl.BlockSpec((blk, d), lambda i, be, nu: (i, 0)),
        scratch_shapes=[pltpu.VMEM((d, hid), BF16),
                        pltpu.VMEM((d, hid), BF16),
                        pltpu.VMEM((hid, d), BF16)],
    )
    return pl.pallas_call(
        _experts_kernel,
        grid_spec=grid_spec,
        out_shape=jax.ShapeDtypeStruct((nb * blk, d), F32),
        compiler_params=_params("arbitrary"),
        name="moe_experts",
    )(block_exp, n_used, xs, w1, w3, w2)


def _combine_kernel(dc_ref, dn_ref, x_ref, gt_ref, gate_ref, fn_ref, ys_hbm, o_ref, ybuf, sem, *, final):
    tm = x_ref.shape[0]
    n_rows = MOE_TOPK * tm
    i = pl.program_id(0)
    nt = pl.num_programs(0)
    slot = lax.rem(i, 2)

    def row_copy(dst_ref, j, s):
        row = 0 if dst_ref is None else dst_ref[0, 0, j]
        return pltpu.make_async_copy(ys_hbm.at[pl.ds(row, 1), :], ybuf.at[s, pl.ds(j, 1), :], sem.at[s])

    def start_all(dst_ref, s):
        def body(j, c):
            row_copy(dst_ref, j, s).start()
            return c
        lax.fori_loop(0, n_rows, body, 0, unroll=8)

    @pl.when(i == 0)
    def _():
        start_all(dc_ref, 0)

    def wait_body(j, c):
        row_copy(None, j, slot).wait()
        return c
    lax.fori_loop(0, n_rows, wait_body, 0, unroll=8)

    @pl.when(i + 1 < nt)
    def _():
        start_all(dn_ref, 1 - slot)

    gt = gt_ref[...]
    y = gt[:, 0:1] * ybuf[slot, pl.ds(0, tm), :] + gt[:, 1:2] * ybuf[slot, pl.ds(tm, tm), :]
    x = x_ref[...] + gate_ref[0] * y
    if final:
        x = x * lax.rsqrt(jnp.mean(x * x, axis=-1, keepdims=True) + EPS) * fn_ref[...]
    o_ref[...] = x


def _combine(x, ys, dest_tiles, gates, gate, final_gain, rows_per_batch, final):
    t, d = x.shape
    bsz = gate.shape[0]
    nt, _, two_tm = dest_tiles.shape
    tm = two_tm // MOE_TOPK
    tpb = rows_per_batch // tm
    smem = functools.partial(pl.BlockSpec, memory_space=pltpu.SMEM)
    return pl.pallas_call(
        functools.partial(_combine_kernel, final=final),
        grid=(nt,),
        in_specs=[smem((1, 1, two_tm), lambda i: (i, 0, 0)),
                  smem((1, 1, two_tm), lambda i: (jnp.minimum(i + 1, nt - 1), 0, 0)),
                  pl.BlockSpec((tm, d), lambda i: (i, 0)),
                  pl.BlockSpec((tm, LANES), lambda i: (i, 0)),
                  pl.BlockSpec((1, 1, d), lambda i: (i // tpb, 0, 0)),
                  pl.BlockSpec((1, d), lambda i: (0, 0)),
                  pl.BlockSpec(memory_space=pl.ANY)],
        out_specs=pl.BlockSpec((tm, d), lambda i: (i, 0)),
        out_shape=jax.ShapeDtypeStruct((t, d), F32),
        scratch_shapes=[pltpu.VMEM((2, two_tm, d), F32),
                        pltpu.SemaphoreType.DMA((2,))],
        compiler_params=_params("arbitrary"),
        name="moe_combine",
    )(dest_tiles, dest_tiles, x, gates, gate.reshape(bsz, 1, d), final_gain.reshape(1, d), ys)


def _moe(x, gain, shift, scale, gate, w_grp, b_grp, w_exp, b_exp, w1, w3, w2, layer, final_gain, seq, final):
    t, _ = x.shape
    tile = _pick(seq, MOE_TOKEN_TILE)
    h2, meta, gts, cnt = _router(x, gain, shift, scale, w_grp, b_grp, w_exp, b_exp, seq)
    dest_tiles, block_exp, n_used, fill = _dispatch_tables(
        meta[:, :MOE_TOPK], meta[:, MOE_TOPK:2 * MOE_TOPK], cnt[0, :MOE_EXPERTS], t // tile, tile)
    xs = _dispatch(h2, dest_tiles, fill, block_exp.shape[0] * MOE_BLOCK)
    ys = _experts(xs, block_exp, n_used, w1, w3, w2, layer)
    return _combine(x, ys, dest_tiles, gts, gate, final_gain, seq, final)


def kernel(x, c, norm_mix, norm_ffn, ada_w, ada_b, gla_w_in, gla_b_r, gla_w_a_up, gla_b_a, gla_norm_g, gla_w_out, sgu_w_in, sgu_b_in, sgu_ln_g, sgu_ln_b, sgu_w_s, sgu_b_s, sgu_w_out, moe_w_grp, moe_b_grp, moe_w_exp, moe_b_exp, moe_w1, moe_w3, moe_w2, final_norm):
    bsz, seq, d = x.shape
    depth = norm_mix.shape[0]
    xt = x.reshape(bsz * seq, d)
    mod = _adaln(c, ada_w, ada_b)
    for i in range(depth):
        sh1, sc1, g1, sh2, sc2, g2 = [mod[i, :, m * d:(m + 1) * d] for m in range(N_MOD)]
        j = i // 2
        if i % 2 == 0:
            w_in = gla_w_in[j]
            n_main = w_in.shape[1] - GLA_RANK
            w_main = w_in[:, :n_main].astype(BF16)
            w_low = jnp.zeros((d, LANES), F32).at[:, :GLA_RANK].set(w_in[:, n_main:]).astype(BF16)
            proj = _norm_matmul(xt, norm_mix[i], sh1, sc1, w_main, None, BF16, seq, "gla_in_proj")
            a_low = _norm_matmul(xt, norm_mix[i], sh1, sc1, w_low, None, F32, seq, "gla_low_proj")
            o = _gla_scan(proj, a_low, gla_w_a_up[j], gla_b_a[j], gla_b_r[j], gla_norm_g[j], bsz, seq)
            xt = _matmul_residual(o, gla_w_out[j].astype(BF16), xt, g1, seq, "gla_out_proj")
        else:
            z = _norm_matmul(xt, norm_mix[i], sh1, sc1, sgu_w_in[j].astype(BF16), sgu_b_in[j], BF16,
                             seq, "sgu_in_proj")
            o = _sgu_spatial(z, sgu_ln_g[j], sgu_ln_b[j], sgu_w_s[j], sgu_b_s[j])
            xt = _matmul_residual(o, sgu_w_out[j].astype(BF16), xt, g1, seq, "sgu_out_proj", tm=512)
        xt = _moe(xt, norm_ffn[i], sh2, sc2, g2, moe_w_grp[i], moe_b_grp[i], moe_w_exp[i], moe_b_exp[i],
                  moe_w1, moe_w3, moe_w2, i, final_norm, seq, final=(i == depth - 1))
    return xt.reshape(bsz, seq, d)
```

```python
import functools
import math

import jax
import jax.numpy as jnp
import numpy as np
from jax import lax
from jax.experimental import pallas as pl
from jax.experimental.pallas import tpu as pltpu

F32 = jnp.float32
BF16 = jnp.bfloat16

EPS = 1e-6
N_MOD = 6
GLA_HEADS = 4
GLA_RANK = 16
GLA_TAU = 16.0
GLA_CHUNK = 256
GLA_FLAT = 8
SGU_CHUNK = 128
SGU_GROUPS = 8
MOE_GROUPS = 8
MOE_PER_GROUP = 8
MOE_EXPERTS = MOE_GROUPS * MOE_PER_GROUP
MOE_TOPK = 2
MOE_BLOCK = 256
MOE_TOKEN_TILE = 256
LANES = 128
SUBLANES = 8
V7X_VMEM_LIMIT_BYTES = 56 * 2**20


def _params(*semantics):
    return pltpu.CompilerParams(dimension_semantics=semantics,
                                vmem_limit_bytes=V7X_VMEM_LIMIT_BYTES)


def _dot(a, b):
    return jnp.dot(a, b, preferred_element_type=F32)


def _dot_nt(a, b):
    return lax.dot_general(a, b, (((1,), (1,)), ((), ())), preferred_element_type=F32)


def _dot_tn(a, b):
    return lax.dot_general(a, b, (((0,), (0,)), ((), ())), preferred_element_type=F32)


def _split3(x):
    hi = x.astype(BF16)
    r1 = x - hi.astype(F32)
    mid = r1.astype(BF16)
    lo = (r1 - mid.astype(F32)).astype(BF16)
    return hi, mid, lo


def _sigmoid(x):
    return 1.0 / (1.0 + jnp.exp(-x))


def _gelu_tanh(x):
    c = math.sqrt(2.0 / math.pi)
    return x * (0.5 * (1.0 + jnp.tanh(c * (x + 0.044715 * (x * x * x)))))


def _pick(n, pref):
    t = min(n, pref)
    while n % t:
        t //= 2
    return t


def _adaln_kernel(c_ref, w_ref, b_ref, o_ref):
    c = c_ref[...]
    ca = c * _sigmoid(c)
    hi, mid, _ = _split3(ca)
    w = w_ref[0].astype(BF16)
    o_ref[0] = _dot(hi, w) + _dot(mid, w) + b_ref[0]


def _adaln(c, ada_w, ada_b):
    depth, d, n = ada_w.shape
    b = c.shape[0]
    rows = 8
    cp = jnp.zeros((rows, d), F32).at[:b].set(c)
    tn = _pick(n, 1024)
    out = pl.pallas_call(
        _adaln_kernel,
        grid=(depth, n // tn),
        in_specs=[pl.BlockSpec((rows, d), lambda l, j: (0, 0)),
                  pl.BlockSpec((1, d, tn), lambda l, j: (l, 0, j)),
                  pl.BlockSpec((1, 1, tn), lambda l, j: (l, 0, j))],
        out_specs=pl.BlockSpec((1, rows, tn), lambda l, j: (l, 0, j)),
        out_shape=jax.ShapeDtypeStruct((depth, rows, n), F32),
        compiler_params=_params("parallel", "parallel"),
        name="adaln",
    )(cp, ada_w, ada_b.reshape(depth, 1, n))
    return out[:, :b]


def _norm_mod(x, gain, shift, scale):
    ms = jnp.mean(x * x, axis=-1, keepdims=True)
    return (x * lax.rsqrt(ms + EPS) * gain) * (1.0 + scale) + shift


def _norm_matmul_kernel(x_ref, g_ref, sh_ref, sc_ref, w_ref, *rest, gelu):
    if gelu:
        bias_ref, o_ref, h_scr = rest
    else:
        o_ref, h_scr = rest

    @pl.when(pl.program_id(1) == 0)
    def _():
        h_scr[...] = _norm_mod(x_ref[...], g_ref[...], sh_ref[0], sc_ref[0]).astype(BF16)

    acc = _dot(h_scr[...], w_ref[...])
    if gelu:
        acc = _gelu_tanh(acc + bias_ref[...])
    o_ref[...] = acc.astype(o_ref.dtype)


def _norm_matmul(x, gain, shift, scale, w, bias, out_dtype, rows_per_batch, name, tm=1024, tn=1024):
    t, d = x.shape
    n = w.shape[1]
    tm = _pick(rows_per_batch, tm)
    tn = _pick(n, tn)
    tpb = rows_per_batch // tm
    bsz = shift.shape[0]
    in_specs = [pl.BlockSpec((tm, d), lambda i, j: (i, 0)),
                pl.BlockSpec((1, d), lambda i, j: (0, 0)),
                pl.BlockSpec((1, 1, d), lambda i, j: (i // tpb, 0, 0)),
                pl.BlockSpec((1, 1, d), lambda i, j: (i // tpb, 0, 0)),
                pl.BlockSpec((d, tn), lambda i, j: (0, j))]
    args = [x, gain.reshape(1, d), shift.reshape(bsz, 1, d), scale.reshape(bsz, 1, d), w]
    if bias is not None:
        in_specs.append(pl.BlockSpec((1, tn), lambda i, j: (0, j)))
        args.append(bias.reshape(1, n))
    return pl.pallas_call(
        functools.partial(_norm_matmul_kernel, gelu=bias is not None),
        grid=(t // tm, n // tn),
        in_specs=in_specs,
        out_specs=pl.BlockSpec((tm, tn), lambda i, j: (i, j)),
        out_shape=jax.ShapeDtypeStruct((t, n), out_dtype),
        scratch_shapes=[pltpu.VMEM((tm, d), BF16)],
        compiler_params=_params("parallel", "arbitrary"),
        name=name,
    )(*args)


def _matmul_residual_kernel(a_ref, w_ref, x_ref, gate_ref, o_ref):
    o_ref[...] = x_ref[...] + gate_ref[0] * _dot(a_ref[...], w_ref[...])


def _matmul_residual(a, w, x, gate, rows_per_batch, name, tm=1024, tn=1024):
    t, k = a.shape
    n = w.shape[1]
    tm = _pick(rows_per_batch, tm)
    tn = _pick(n, tn)
    tpb = rows_per_batch // tm
    bsz = gate.shape[0]
    return pl.pallas_call(
        _matmul_residual_kernel,
        grid=(t // tm, n // tn),
        in_specs=[pl.BlockSpec((tm, k), lambda i, j: (i, 0)),
                  pl.BlockSpec((k, tn), lambda i, j: (0, j)),
                  pl.BlockSpec((tm, tn), lambda i, j: (i, j)),
                  pl.BlockSpec((1, 1, tn), lambda i, j: (i // tpb, 0, j))],
        out_specs=pl.BlockSpec((tm, tn), lambda i, j: (i, j)),
        out_shape=jax.ShapeDtypeStruct((t, n), F32),
        compiler_params=_params("parallel", "parallel"),
        name=name,
    )(a, w, x, gate.reshape(bsz, 1, n))


def _rows_bcast(b, n, off):
    c, dk = b.shape
    parts = [jnp.broadcast_to(b[j * n + off:j * n + off + 1, :], (n, dk)) for j in range(c // n)]
    return parts[0] if len(parts) == 1 else jnp.concatenate(parts, axis=0)


def _pair_levels(c):
    t = np.arange(c)[:, None]
    s = np.arange(c)[None, :]
    x = t ^ s
    lvl = np.floor(np.log2(np.maximum(x, 1))).astype(np.int32) + 1
    lvl = np.where(x < GLA_FLAT, 0, lvl)
    return np.where(s <= t, lvl, -1).astype(np.int32)


def _gla_scan_kernel(q_ref, k_ref, v_ref, r_ref, al_ref, wup_ref, ba_ref, br_ref, ng_ref, lvl_ref,
                     o_ref, st_ref, *, scale):
    c, dk = q_ref.shape

    @pl.when(pl.program_id(2) == 0)
    def _():
        st_ref[...] = jnp.zeros_like(st_ref)

    q = q_ref[...].astype(F32) * scale
    k = k_ref[...].astype(F32)
    v = v_ref[...]

    al = _split3(al_ref[:, :GLA_RANK])
    wu = _split3(wup_ref[...])
    pre = _dot(al[0], wu[0]) + _dot(al[0], wu[1]) + _dot(al[1], wu[0]) + ba_ref[...]
    g = (jnp.minimum(pre, 0.0) - jnp.log(1.0 + jnp.exp(-jnp.abs(pre)))) * (1.0 / GLA_TAU)

    lvl = lvl_ref[...]
    tri = jnp.where(lvl >= 0, 1.0, 0.0).astype(BF16)
    gs = _split3(g)
    b = _dot(tri, gs[0]) + _dot(tri, gs[1]) + _dot(tri, gs[2])

    a = jnp.zeros((c, c), F32)
    n = c
    while n > GLA_FLAT:
        e = jnp.exp(-jnp.abs(b - _rows_bcast(b, n, n // 2)))
        p = _dot_nt((q * e).astype(BF16), (k * e).astype(BF16))
        a = jnp.where(lvl == n.bit_length() - 1, p, a)
        n //= 2
    ref = _rows_bcast(b, GLA_FLAT, 0)
    qh = (q * jnp.exp(b - ref)).astype(BF16)
    kh = (k * jnp.exp(ref - b)).astype(BF16)
    a = jnp.where(lvl == 0, _dot_nt(qh, kh), a)

    st = st_ref[...]
    o = _dot(a.astype(BF16), v) + _dot_nt((q * jnp.exp(b)).astype(BF16), st.astype(BF16))

    b_last = b[c - 1:c, :]
    kd = (k * jnp.exp(b_last - b)).astype(BF16)
    st_ref[...] = st * jnp.exp(b_last) + _dot_tn(v, kd)

    on = o * lax.rsqrt(jnp.mean(o * o, axis=-1, keepdims=True) + EPS) * ng_ref[...]
    r = r_ref[...].astype(F32) + br_ref[...]
    o_ref[...] = (r * _sigmoid(r) * on).astype(o_ref.dtype)


def _gla_scan(proj, a_low, w_a_up, b_a, b_r, norm_g, bsz, seq):
    t, n3 = proj.shape
    dkk = n3 // 6
    dvv = 2 * dkk
    h = GLA_HEADS
    dk, dv = dkk // h, dvv // h
    c = _pick(seq, GLA_CHUNK)
    nc = seq // c
    voff = (2 * dkk) // dv
    roff = (2 * dkk + dvv) // dv
    rowmap = lambda b, hh, n: b * nc + n
    return pl.pallas_call(
        functools.partial(_gla_scan_kernel, scale=float(dk) ** -0.5),
        grid=(bsz, h, nc),
        in_specs=[pl.BlockSpec((c, dk), lambda b, hh, n: (rowmap(b, hh, n), hh)),
                  pl.BlockSpec((c, dk), lambda b, hh, n: (rowmap(b, hh, n), h + hh)),
                  pl.BlockSpec((c, dv), lambda b, hh, n: (rowmap(b, hh, n), voff + hh)),
                  pl.BlockSpec((c, dv), lambda b, hh, n: (rowmap(b, hh, n), roff + hh)),
                  pl.BlockSpec((c, LANES), lambda b, hh, n: (rowmap(b, hh, n), 0)),
                  pl.BlockSpec((GLA_RANK, dk), lambda b, hh, n: (0, hh)),
                  pl.BlockSpec((1, dk), lambda b, hh, n: (0, hh)),
                  pl.BlockSpec((1, dv), lambda b, hh, n: (0, hh)),
                  pl.BlockSpec((1, dv), lambda b, hh, n: (0, 0)),
                  pl.BlockSpec((c, c), lambda b, hh, n: (0, 0))],
        out_specs=pl.BlockSpec((c, dv), lambda b, hh, n: (rowmap(b, hh, n), hh)),
        out_shape=jax.ShapeDtypeStruct((t, dvv), BF16),
        scratch_shapes=[pltpu.VMEM((dv, dk), F32)],
        compiler_params=_params("parallel", "parallel", "arbitrary"),
        name="gla_scan",
    )(proj, proj, proj, proj, a_low, w_a_up, b_a.reshape(1, dkk), b_r.reshape(1, dvv),
      norm_g.reshape(1, dv), jnp.asarray(_pair_levels(c)))


def _sgu_spatial_kernel(u_ref, v_ref, lg_ref, lb_ref, ws_ref, bs_ref, o_ref, wc_scr):
    rows, half = v_ref.shape
    c = SGU_CHUNK
    gw = half // SGU_GROUPS

    @pl.when(pl.program_id(0) == 0)
    def _():
        row = lax.broadcasted_iota(jnp.int32, (c, c), 0)
        col = lax.broadcasted_iota(jnp.int32, (c, c), 1)
        for g in range(SGU_GROUPS):
            wc_scr[g] = jnp.where(row >= col, ws_ref[g], 0.0).astype(BF16)

    for ci in range(rows // c):
        rs = pl.ds(ci * c, c)
        v = v_ref[rs, :].astype(F32)
        mu = jnp.mean(v, axis=-1, keepdims=True)
        vc = v - mu
        var = jnp.mean(vc * vc, axis=-1, keepdims=True)
        vn = (vc * lax.rsqrt(var + EPS) * lg_ref[...] + lb_ref[...]).astype(BF16)
        for g in range(SGU_GROUPS):
            cs = pl.ds(g * gw, gw)
            mixed = _dot(wc_scr[g], vn[:, g * gw:(g + 1) * gw]) + bs_ref[:, g:g + 1]
            o_ref[rs, cs] = (u_ref[rs, cs].astype(F32) * mixed).astype(o_ref.dtype)


def _sgu_spatial(z, ln_g, ln_b, w_s, b_s, rows=512):
    t, two_half = z.shape
    half = two_half // 2
    c = SGU_CHUNK
    rows = max(c, _pick(t, rows))
    return pl.pallas_call(
        _sgu_spatial_kernel,
        grid=(t // rows,),
        in_specs=[pl.BlockSpec((rows, half), lambda i: (i, 0)),
                  pl.BlockSpec((rows, half), lambda i: (i, 1)),
                  pl.BlockSpec((1, half), lambda i: (0, 0)),
                  pl.BlockSpec((1, half), lambda i: (0, 0)),
                  pl.BlockSpec((SGU_GROUPS, c, c), lambda i: (0, 0, 0)),
                  pl.BlockSpec((c, SGU_GROUPS), lambda i: (0, 0))],
        out_specs=pl.BlockSpec((rows, half), lambda i: (i, 0)),
        out_shape=jax.ShapeDtypeStruct((t, half), BF16),
        scratch_shapes=[pltpu.VMEM((SGU_GROUPS, c, c), BF16)],
        compiler_params=_params("arbitrary"),
        name="sgu_spatial",
    )(z, z, ln_g.reshape(1, half), ln_b.reshape(1, half), w_s, b_s.T)


def _router_kernel(x_ref, g_ref, sh_ref, sc_ref, w_ref, b_ref, h_ref, id_ref, gt_ref, cnt_ref, cnt_scr):
    @pl.when(pl.program_id(0) == 0)
    def _():
        cnt_scr[...] = jnp.zeros_like(cnt_scr)

    h = _norm_mod(x_ref[...], g_ref[...], sh_ref[0], sc_ref[0])
    h_ref[...] = h
    hi, mid, _ = _split3(h)
    w = w_ref[...]
    l2 = _dot(hi, w) + _dot(mid, w)
    logits = l2[:, :LANES] + l2[:, LANES:] + b_ref[...]
    lane = lax.broadcasted_iota(jnp.int32, logits.shape, 1)
    lane_f = lane.astype(F32)
    neg = -jnp.inf
    big = float(LANES)

    grp = jnp.where(lane < MOE_GROUPS, logits, neg)
    gmax = jnp.max(grp, axis=-1, keepdims=True)
    gidx = jnp.min(jnp.where(grp == gmax, lane_f, big), axis=-1, keepdims=True)
    g_w = 1.0 / jnp.sum(jnp.exp(grp - gmax), axis=-1, keepdims=True)

    e_lane = lane - MOE_GROUPS
    in_grp = (e_lane >= 0) & (e_lane < MOE_EXPERTS) & ((e_lane // MOE_PER_GROUP).astype(F32) == gidx)
    el = jnp.where(in_grp, logits, neg)
    m1 = jnp.max(el, axis=-1, keepdims=True)
    i1 = jnp.min(jnp.where(el == m1, lane_f, big), axis=-1, keepdims=True)
    el2 = jnp.where(lane_f == i1, neg, el)
    m2 = jnp.max(el2, axis=-1, keepdims=True)
    i2 = jnp.min(jnp.where(el2 == m2, lane_f, big), axis=-1, keepdims=True)
    z = jnp.sum(jnp.exp(el - m1), axis=-1, keepdims=True)
    p1 = 1.0 / z
    p2 = jnp.exp(m2 - m1) / z
    psum = p1 + p2
    e1 = i1 - float(MOE_GROUPS)
    e2 = i2 - float(MOE_GROUPS)

    tm = logits.shape[0]
    oh1 = (lane_f == e1).astype(F32)
    oh2 = (lane_f == e2).astype(F32)
    oh = oh1 + oh2
    row = lax.broadcasted_iota(jnp.int32, (tm, tm), 0)
    col = lax.broadcasted_iota(jnp.int32, (tm, tm), 1)
    before = _dot((row > col).astype(BF16), oh.astype(BF16)) + cnt_scr[...]
    r1 = jnp.sum(oh1 * before, axis=-1, keepdims=True)
    r2 = jnp.sum(oh2 * before, axis=-1, keepdims=True)
    cnt = cnt_scr[...] + jnp.sum(oh, axis=0, keepdims=True)
    cnt_scr[...] = cnt
    cnt_ref[...] = jnp.broadcast_to(cnt, cnt_ref.shape).astype(jnp.int32)

    meta = jnp.where(lane == 0, e1, jnp.where(lane == 1, e2, jnp.where(lane == 2, r1, jnp.where(lane == 3, r2, 0.0))))
    id_ref[...] = meta.astype(jnp.int32)
    gt_ref[...] = jnp.where(lane == 0, g_w * (p1 / psum), jnp.where(lane == 1, g_w * (p2 / psum), 0.0))


def _router(x, gain, shift, scale, w_grp, b_grp, w_exp, b_exp, rows_per_batch, tm=512):
    t, d = x.shape
    bsz = shift.shape[0]
    tm = _pick(rows_per_batch, tm)
    tpb = rows_per_batch // tm
    n_log = MOE_GROUPS + MOE_EXPERTS
    wcat = jnp.zeros((d, LANES), F32).at[:, :MOE_GROUPS].set(w_grp).at[:, MOE_GROUPS:n_log].set(w_exp)
    w_hi = wcat.astype(BF16)
    w_lo = (wcat - w_hi.astype(F32)).astype(BF16)
    w2 = jnp.concatenate([w_hi, w_lo], axis=1)
    bcat = jnp.zeros((1, LANES), F32).at[0, :MOE_GROUPS].set(b_grp).at[0, MOE_GROUPS:n_log].set(b_exp)
    return pl.pallas_call(
        _router_kernel,
        grid=(t // tm,),
        in_specs=[pl.BlockSpec((tm, d), lambda i: (i, 0)),
                  pl.BlockSpec((1, d), lambda i: (0, 0)),
                  pl.BlockSpec((1, 1, d), lambda i: (i // tpb, 0, 0)),
                  pl.BlockSpec((1, 1, d), lambda i: (i // tpb, 0, 0)),
                  pl.BlockSpec((d, 2 * LANES), lambda i: (0, 0)),
                  pl.BlockSpec((1, LANES), lambda i: (0, 0))],
        out_specs=[pl.BlockSpec((tm, d), lambda i: (i, 0)),
                   pl.BlockSpec((tm, LANES), lambda i: (i, 0)),
                   pl.BlockSpec((tm, LANES), lambda i: (i, 0)),
                   pl.BlockSpec((8, LANES), lambda i: (0, 0))],
        out_shape=[jax.ShapeDtypeStruct((t, d), F32),
                   jax.ShapeDtypeStruct((t, LANES), jnp.int32),
                   jax.ShapeDtypeStruct((t, LANES), F32),
                   jax.ShapeDtypeStruct((8, LANES), jnp.int32)],
        scratch_shapes=[pltpu.VMEM((1, LANES), F32)],
        compiler_params=_params("arbitrary"),
        name="moe_router",
    )(x, gain.reshape(1, d), shift.reshape(bsz, 1, d), scale.reshape(bsz, 1, d), w2, bcat)


def _dispatch_tables(ids, rank, counts, n_tiles, tile):
    blk = MOE_BLOCK
    a = ids.shape[0] * MOE_TOPK
    padded = (counts + blk - 1) // blk * blk
    pad_ends = jnp.cumsum(padded)
    pad_starts = pad_ends - padded
    onehot = ids[:, :, None] == jnp.arange(MOE_EXPERTS, dtype=jnp.int32)
    dest = jnp.sum(jnp.where(onehot, pad_starts, 0), axis=-1) + rank
    dest_tiles = dest.reshape(n_tiles, tile, MOE_TOPK).transpose(0, 2, 1).reshape(n_tiles, 1, MOE_TOPK * tile)
    nb = -(-(a + MOE_EXPERTS * (blk - 1)) // blk)
    block_start = jnp.arange(nb, dtype=jnp.int32) * blk
    block_exp = jnp.minimum(jnp.sum(pad_ends[None, :] <= block_start[:, None], axis=1), MOE_EXPERTS - 1)
    n_used = (pad_ends[-1] // blk).reshape(1)
    fill = jnp.stack([pad_starts + counts, padded - counts,
                      jnp.broadcast_to(n_used, (MOE_EXPERTS,))])
    return dest_tiles.astype(jnp.int32), block_exp.astype(jnp.int32), n_used.astype(jnp.int32), fill.astype(jnp.int32)


def _dispatch_kernel(fill_ref, dest_ref, h_ref, xs_hbm, zbuf, sem, zsem):
    tm = h_ref.shape[0]
    i = pl.program_id(0)

    def fill_copies(start):
        def issue(cp, on):
            @pl.when(on)
            def _():
                cp.start() if start else cp.wait()

        for e in range(MOE_EXPERTS):
            off = fill_ref[0, e]
            n_pad = fill_ref[1, e]
            head = n_pad & (SUBLANES - 1)
            for hrow in range(SUBLANES - 1):
                issue(pltpu.make_async_copy(zbuf.at[pl.ds(0, 1), :], xs_hbm.at[pl.ds(off + hrow, 1), :], zsem),
                      hrow < head)
            off = off + head
            bit = MOE_BLOCK // 2
            while bit >= SUBLANES:
                issue(pltpu.make_async_copy(zbuf.at[pl.ds(0, bit), :],
                                            xs_hbm.at[pl.ds(pl.multiple_of(off, SUBLANES), bit), :], zsem),
                      (n_pad & bit) != 0)
                off = off + (n_pad & bit)
                bit //= 2

        half = zbuf.shape[0]
        first = fill_ref[2, 0] * (MOE_BLOCK // half)

        def tail(j, c):
            cp = pltpu.make_async_copy(zbuf, xs_hbm.at[pl.ds(pl.multiple_of(j * half, SUBLANES), half), :], zsem)
            cp.start() if start else cp.wait()
            return c
        lax.fori_loop(first, xs_hbm.shape[0] // half, tail, 0)

    @pl.when(i == 0)
    def _():
        zbuf[...] = jnp.zeros_like(zbuf)
        fill_copies(True)
        fill_copies(False)

    def row_copy(j, dst):
        return pltpu.make_async_copy(h_ref.at[pl.ds(j % tm, 1), :], xs_hbm.at[pl.ds(dst, 1), :], sem)

    for j in range(MOE_TOPK * tm):
        row_copy(j, dest_ref[0, 0, j]).start(priority=j % 2)

    def wait(j, c):
        row_copy(0, 0).wait()
        return c
    lax.fori_loop(0, MOE_TOPK * tm, wait, 0, unroll=8)


def _dispatch(h2, dest_tiles, fill, n_rows):
    t, d = h2.shape
    nt, _, two_tm = dest_tiles.shape
    tm = two_tm // MOE_TOPK
    grid_spec = pltpu.PrefetchScalarGridSpec(
        num_scalar_prefetch=1,
        grid=(nt,),
        in_specs=[pl.BlockSpec((1, 1, two_tm), lambda i, f: (i, 0, 0), memory_space=pltpu.SMEM),
                  pl.BlockSpec((tm, d), lambda i, f: (i, 0))],
        out_specs=pl.BlockSpec(memory_space=pl.ANY),
        scratch_shapes=[pltpu.VMEM((MOE_BLOCK // 2, d), F32),
                        pltpu.SemaphoreType.DMA(()),
                        pltpu.SemaphoreType.DMA(())],
    )
    return pl.pallas_call(
        _dispatch_kernel,
        grid_spec=grid_spec,
        out_shape=jax.ShapeDtypeStruct((n_rows, d), F32),
        compiler_params=_params("arbitrary"),
        name="moe_dispatch",
    )(fill, dest_tiles, h2)


def _experts_kernel(bexp_ref, nused_ref, x_ref, w1_ref, w3_ref, w2_ref, y_ref, w1b, w3b, w2b):
    i = pl.program_id(0)
    used = i < nused_ref[0]
    e = bexp_ref[i]
    e_prev = bexp_ref[jnp.maximum(i - 1, 0)]

    @pl.when(used & ((i == 0) | (e != e_prev)))
    def _():
        w1b[...] = w1_ref[0, 0].astype(BF16)
        w3b[...] = w3_ref[0, 0].astype(BF16)
        w2b[...] = w2_ref[0, 0].astype(BF16)

    @pl.when(used)
    def _():
        x = x_ref[...].astype(BF16)
        a = _dot(x, w1b[...])
        hb = (a * _sigmoid(a)) * _dot(x, w3b[...])
        y_ref[...] = _dot(hb.astype(BF16), w2b[...])

    @pl.when(jnp.logical_not(used))
    def _():
        y_ref[...] = jnp.zeros_like(y_ref)


def _experts(xs, block_exp, n_used, w1, w3, w2, layer):
    nb = block_exp.shape[0]
    d = xs.shape[1]
    hid = w1.shape[-1]
    blk = MOE_BLOCK

    def xmap(i, be, nu):
        return (jnp.minimum(i, jnp.maximum(nu[0], 1) - 1), 0)

    def wmap(i, be, nu):
        return (layer, be[i], 0, 0)

    grid_spec = pltpu.PrefetchScalarGridSpec(
        num_scalar_prefetch=2,
        grid=(nb,),
        in_specs=[pl.BlockSpec((blk, d), xmap),
                  pl.BlockSpec((1, 1, d, hid), wmap),
                  pl.BlockSpec((1, 1, d, hid), wmap),
                  pl.BlockSpec((1, 1, hid, d), wmap)],
        out_specs=pl.BlockSpec((blk, d), lambda i, be, nu: (i, 0)),
        scratch_shapes=[pltpu.VMEM((d, hid), BF16),
                        pltpu.VMEM((d, hid), BF16),
                        pltpu.VMEM((hid, d), BF16)],
    )
    return pl.pallas_call(
        _experts_kernel,
        grid_spec=grid_spec,
        out_shape=jax.ShapeDtypeStruct((nb * blk, d), F32),
        compiler_params=_params("arbitrary"),
        name="moe_experts",
    )(block_exp, n_used, xs, w1, w3, w2)


def _combine_kernel(dc_ref, dn_ref, x_ref, gt_ref, gate_ref, fn_ref, ys_hbm, o_ref, ybuf, sem, *, final):
    tm = x_ref.shape[0]
    n_rows = MOE_TOPK * tm
    i = pl.program_id(0)
    nt = pl.num_programs(0)
    slot = lax.rem(i, 2)

    def row_copy(row, j, s):
        return pltpu.make_async_copy(ys_hbm.at[pl.ds(row, 1), :], ybuf.at[s, pl.ds(j, 1), :], sem.at[s])

    def start_all(dst_ref, s):
        for j in range(n_rows):
            row_copy(dst_ref[0, 0, j], j, s).start(priority=j % 2)

    @pl.when(i == 0)
    def _():
        start_all(dc_ref, 0)

    def wait_body(j, c):
        row_copy(0, 0, slot).wait()
        return c
    lax.fori_loop(0, n_rows, wait_body, 0, unroll=8)

    for s in range(2):
        @pl.when((i + 1 < nt) & (slot == 1 - s))
        def _(s=s):
            start_all(dn_ref, s)

    gt = gt_ref[...]
    y = gt[:, 0:1] * ybuf[slot, pl.ds(0, tm), :] + gt[:, 1:2] * ybuf[slot, pl.ds(tm, tm), :]
    x = x_ref[...] + gate_ref[0] * y
    if final:
        x = x * lax.rsqrt(jnp.mean(x * x, axis=-1, keepdims=True) + EPS) * fn_ref[...]
    o_ref[...] = x


def _combine(x, ys, dest_tiles, gates, gate, final_gain, rows_per_batch, final):
    t, d = x.shape
    bsz = gate.shape[0]
    nt, _, two_tm = dest_tiles.shape
    tm = two_tm // MOE_TOPK
    tpb = rows_per_batch // tm
    smem = functools.partial(pl.BlockSpec, memory_space=pltpu.SMEM)
    return pl.pallas_call(
        functools.partial(_combine_kernel, final=final),
        grid=(nt,),
        in_specs=[smem((1, 1, two_tm), lambda i: (i, 0, 0)),
                  smem((1, 1, two_tm), lambda i: (jnp.minimum(i + 1, nt - 1), 0, 0)),
                  pl.BlockSpec((tm, d), lambda i: (i, 0)),
                  pl.BlockSpec((tm, LANES), lambda i: (i, 0)),
                  pl.BlockSpec((1, 1, d), lambda i: (i // tpb, 0, 0)),
                  pl.BlockSpec((1, d), lambda i: (0, 0)),
                  pl.BlockSpec(memory_space=pl.ANY)],
        out_specs=pl.BlockSpec((tm, d), lambda i: (i, 0)),
        out_shape=jax.ShapeDtypeStruct((t, d), F32),
        scratch_shapes=[pltpu.VMEM((2, two_tm, d), F32),
                        pltpu.SemaphoreType.DMA((2,))],
        compiler_params=_params("arbitrary"),
        name="moe_combine",
    )(dest_tiles, dest_tiles, x, gates, gate.reshape(bsz, 1, d), final_gain.reshape(1, d), ys)


def _moe(x, gain, shift, scale, gate, w_grp, b_grp, w_exp, b_exp, w1, w3, w2, layer, final_gain, seq, final):
    t, _ = x.shape
    tile = _pick(seq, MOE_TOKEN_TILE)
    h2, meta, gts, cnt = _router(x, gain, shift, scale, w_grp, b_grp, w_exp, b_exp, seq)
    dest_tiles, block_exp, n_used, fill = _dispatch_tables(
        meta[:, :MOE_TOPK], meta[:, MOE_TOPK:2 * MOE_TOPK], cnt[0, :MOE_EXPERTS], t // tile, tile)
    xs = _dispatch(h2, dest_tiles, fill, block_exp.shape[0] * MOE_BLOCK)
    ys = _experts(xs, block_exp, n_used, w1, w3, w2, layer)
    return _combine(x, ys, dest_tiles, gts, gate, final_gain, seq, final)


def kernel(x, c, norm_mix, norm_ffn, ada_w, ada_b, gla_w_in, gla_b_r, gla_w_a_up, gla_b_a, gla_norm_g, gla_w_out, sgu_w_in, sgu_b_in, sgu_ln_g, sgu_ln_b, sgu_w_s, sgu_b_s, sgu_w_out, moe_w_grp, moe_b_grp, moe_w_exp, moe_b_exp, moe_w1, moe_w3, moe_w2, final_norm):
    bsz, seq, d = x.shape
    depth = norm_mix.shape[0]
    xt = x.reshape(bsz * seq, d)
    mod = _adaln(c, ada_w, ada_b)
    for i in range(depth):
        sh1, sc1, g1, sh2, sc2, g2 = [mod[i, :, m * d:(m + 1) * d] for m in range(N_MOD)]
        j = i // 2
        if i % 2 == 0:
            w_in = gla_w_in[j]
            n_main = w_in.shape[1] - GLA_RANK
            w_main = w_in[:, :n_main].astype(BF16)
            w_low = jnp.zeros((d, LANES), F32).at[:, :GLA_RANK].set(w_in[:, n_main:]).astype(BF16)
            proj = _norm_matmul(xt, norm_mix[i], sh1, sc1, w_main, None, BF16, seq, "gla_in_proj")
            a_low = _norm_matmul(xt, norm_mix[i], sh1, sc1, w_low, None, F32, seq, "gla_low_proj")
            o = _gla_scan(proj, a_low, gla_w_a_up[j], gla_b_a[j], gla_b_r[j], gla_norm_g[j], bsz, seq)
            xt = _matmul_residual(o, gla_w_out[j].astype(BF16), xt, g1, seq, "gla_out_proj")
        else:
            z = _norm_matmul(xt, norm_mix[i], sh1, sc1, sgu_w_in[j].astype(BF16), sgu_b_in[j], BF16,
                             seq, "sgu_in_proj")
            o = _sgu_spatial(z, sgu_ln_g[j], sgu_ln_b[j], sgu_w_s[j], sgu_b_s[j])
            xt = _matmul_residual(o, sgu_w_out[j].astype(BF16), xt, g1, seq, "sgu_out_proj", tm=512)
        xt = _moe(xt, norm_ffn[i], sh2, sc2, g2, moe_w_grp[i], moe_b_grp[i], moe_w_exp[i], moe_b_exp[i],
                  moe_w1, moe_w3, moe_w2, i, final_norm, seq, final=(i == depth - 1))
    return xt.reshape(bsz, seq, d)
```

```python
import functools
import math

import jax
import jax.numpy as jnp
import numpy as np
from jax import lax
from jax.experimental import pallas as pl
from jax.experimental.pallas import tpu as pltpu

F32 = jnp.float32
BF16 = jnp.bfloat16

EPS = 1e-6
N_MOD = 6
GLA_HEADS = 4
GLA_RANK = 16
GLA_TAU = 16.0
GLA_CHUNK = 256
GLA_FLAT = 8
SGU_CHUNK = 128
SGU_GROUPS = 8
MOE_GROUPS = 8
MOE_PER_GROUP = 8
MOE_EXPERTS = MOE_GROUPS * MOE_PER_GROUP
MOE_TOPK = 2
MOE_BLOCK = 256
MOE_TOKEN_TILE = 256
LANES = 128
V7X_VMEM_LIMIT_BYTES = 56 * 2**20


def _params(*semantics):
    return pltpu.CompilerParams(dimension_semantics=semantics,
                                vmem_limit_bytes=V7X_VMEM_LIMIT_BYTES)


def _dot(a, b):
    return jnp.dot(a, b, preferred_element_type=F32)


def _dot_nt(a, b):
    return lax.dot_general(a, b, (((1,), (1,)), ((), ())), preferred_element_type=F32)


def _dot_tn(a, b):
    return lax.dot_general(a, b, (((0,), (0,)), ((), ())), preferred_element_type=F32)


def _split3(x):
    hi = x.astype(BF16)
    r1 = x - hi.astype(F32)
    mid = r1.astype(BF16)
    lo = (r1 - mid.astype(F32)).astype(BF16)
    return hi, mid, lo


def _sigmoid(x):
    return 1.0 / (1.0 + jnp.exp(-x))


def _gelu_tanh(x):
    c = math.sqrt(2.0 / math.pi)
    return x * (0.5 * (1.0 + jnp.tanh(c * (x + 0.044715 * (x * x * x)))))


def _pick(n, pref):
    t = min(n, pref)
    while n % t:
        t //= 2
    return t


def _adaln_kernel(c_ref, w_ref, b_ref, o_ref):
    c = c_ref[...]
    ca = c * _sigmoid(c)
    hi, mid, _ = _split3(ca)
    w = w_ref[0].astype(BF16)
    o_ref[0] = _dot(hi, w) + _dot(mid, w) + b_ref[0]


def _adaln(c, ada_w, ada_b):
    depth, d, n = ada_w.shape
    b = c.shape[0]
    rows = 8
    cp = jnp.zeros((rows, d), F32).at[:b].set(c)
    tn = _pick(n, 1024)
    out = pl.pallas_call(
        _adaln_kernel,
        grid=(depth, n // tn),
        in_specs=[pl.BlockSpec((rows, d), lambda l, j: (0, 0)),
                  pl.BlockSpec((1, d, tn), lambda l, j: (l, 0, j)),
                  pl.BlockSpec((1, 1, tn), lambda l, j: (l, 0, j))],
        out_specs=pl.BlockSpec((1, rows, tn), lambda l, j: (l, 0, j)),
        out_shape=jax.ShapeDtypeStruct((depth, rows, n), F32),
        compiler_params=_params("parallel", "parallel"),
        name="adaln",
    )(cp, ada_w, ada_b.reshape(depth, 1, n))
    return out[:, :b]


def _norm_mod(x, gain, shift, scale):
    ms = jnp.mean(x * x, axis=-1, keepdims=True)
    return (x * lax.rsqrt(ms + EPS) * gain) * (1.0 + scale) + shift


def _norm_matmul_kernel(x_ref, g_ref, sh_ref, sc_ref, w_ref, *rest, gelu, side):
    rest = list(rest)
    bias_ref = rest.pop(0) if gelu else None
    wside_ref = rest.pop(0) if side else None
    o_ref = rest.pop(0)
    oside_ref = rest.pop(0) if side else None
    h_scr, = rest

    @pl.when(pl.program_id(1) == 0)
    def _():
        h_scr[...] = _norm_mod(x_ref[...], g_ref[...], sh_ref[0], sc_ref[0]).astype(BF16)
        if side:
            oside_ref[...] = _dot(h_scr[...], wside_ref[...])

    acc = _dot(h_scr[...], w_ref[...])
    if gelu:
        acc = _gelu_tanh(acc + bias_ref[...])
    o_ref[...] = acc.astype(o_ref.dtype)


def _norm_matmul(x, gain, shift, scale, w, rows_per_batch, name, bias=None, w_side=None, tm=1024, tn=1024):
    t, d = x.shape
    n = w.shape[1]
    tm = _pick(rows_per_batch, tm)
    tn = _pick(n, tn)
    tpb = rows_per_batch // tm
    bsz = shift.shape[0]
    in_specs = [pl.BlockSpec((tm, d), lambda i, j: (i, 0)),
                pl.BlockSpec((1, d), lambda i, j: (0, 0)),
                pl.BlockSpec((1, 1, d), lambda i, j: (i // tpb, 0, 0)),
                pl.BlockSpec((1, 1, d), lambda i, j: (i // tpb, 0, 0)),
                pl.BlockSpec((d, tn), lambda i, j: (0, j))]
    args = [x, gain.reshape(1, d), shift.reshape(bsz, 1, d), scale.reshape(bsz, 1, d), w]
    out_specs = [pl.BlockSpec((tm, tn), lambda i, j: (i, j))]
    out_shape = [jax.ShapeDtypeStruct((t, n), BF16)]
    if bias is not None:
        in_specs.append(pl.BlockSpec((1, tn), lambda i, j: (0, j)))
        args.append(bias.reshape(1, n))
    if w_side is not None:
        in_specs.append(pl.BlockSpec((d, LANES), lambda i, j: (0, 0)))
        args.append(w_side)
        out_specs.append(pl.BlockSpec((tm, LANES), lambda i, j: (i, 0)))
        out_shape.append(jax.ShapeDtypeStruct((t, LANES), F32))
    return pl.pallas_call(
        functools.partial(_norm_matmul_kernel, gelu=bias is not None, side=w_side is not None),
        grid=(t // tm, n // tn),
        in_specs=in_specs,
        out_specs=out_specs,
        out_shape=out_shape,
        scratch_shapes=[pltpu.VMEM((tm, d), BF16)],
        compiler_params=_params("parallel", "arbitrary"),
        name=name,
    )(*args)


def _matmul_residual_kernel(a_ref, w_ref, x_ref, gate_ref, o_ref):
    o_ref[...] = x_ref[...] + gate_ref[0] * _dot(a_ref[...], w_ref[...])


def _matmul_residual(a, w, x, gate, rows_per_batch, name, tm=1024, tn=1024):
    t, k = a.shape
    n = w.shape[1]
    tm = _pick(rows_per_batch, tm)
    tn = _pick(n, tn)
    tpb = rows_per_batch // tm
    bsz = gate.shape[0]
    return pl.pallas_call(
        _matmul_residual_kernel,
        grid=(t // tm, n // tn),
        in_specs=[pl.BlockSpec((tm, k), lambda i, j: (i, 0)),
                  pl.BlockSpec((k, tn), lambda i, j: (0, j)),
                  pl.BlockSpec((tm, tn), lambda i, j: (i, j)),
                  pl.BlockSpec((1, 1, tn), lambda i, j: (i // tpb, 0, j))],
        out_specs=pl.BlockSpec((tm, tn), lambda i, j: (i, j)),
        out_shape=jax.ShapeDtypeStruct((t, n), F32),
        compiler_params=_params("parallel", "parallel"),
        name=name,
    )(a, w, x, gate.reshape(bsz, 1, n))


def _rows_bcast(b, n, off):
    c, dk = b.shape
    parts = [jnp.broadcast_to(b[j * n + off:j * n + off + 1, :], (n, dk)) for j in range(c // n)]
    return parts[0] if len(parts) == 1 else jnp.concatenate(parts, axis=0)


def _pair_levels(c):
    t = np.arange(c)[:, None]
    s = np.arange(c)[None, :]
    x = t ^ s
    lvl = np.floor(np.log2(np.maximum(x, 1))).astype(np.int32) + 1
    lvl = np.where(x < GLA_FLAT, 0, lvl)
    return np.where(s <= t, lvl, -1).astype(np.int32)


def _gla_scan_kernel(q_ref, k_ref, v_ref, r_ref, al_ref, wup_ref, ba_ref, br_ref, ng_ref, lvl_ref,
                     o_ref, st_ref, *, scale):
    c, dk = q_ref.shape

    @pl.when(pl.program_id(2) == 0)
    def _():
        st_ref[...] = jnp.zeros_like(st_ref)

    q = q_ref[...].astype(F32) * scale
    k = k_ref[...].astype(F32)
    v = v_ref[...]

    al = _split3(al_ref[:, :GLA_RANK])
    wu = _split3(wup_ref[...])
    pre = _dot(al[0], wu[0]) + _dot(al[0], wu[1]) + _dot(al[1], wu[0]) + ba_ref[...]
    g = (jnp.minimum(pre, 0.0) - jnp.log(1.0 + jnp.exp(-jnp.abs(pre)))) * (1.0 / GLA_TAU)

    lvl = lvl_ref[...]
    tri = jnp.where(lvl >= 0, 1.0, 0.0).astype(BF16)
    gs = _split3(g)
    b = _dot(tri, gs[0]) + _dot(tri, gs[1]) + _dot(tri, gs[2])

    a = jnp.zeros((c, c), F32)
    n = c
    while n > GLA_FLAT:
        e = jnp.exp(-jnp.abs(b - _rows_bcast(b, n, n // 2)))
        p = _dot_nt((q * e).astype(BF16), (k * e).astype(BF16))
        a = jnp.where(lvl == n.bit_length() - 1, p, a)
        n //= 2
    ref = _rows_bcast(b, GLA_FLAT, 0)
    qh = (q * jnp.exp(b - ref)).astype(BF16)
    kh = (k * jnp.exp(ref - b)).astype(BF16)
    a = jnp.where(lvl == 0, _dot_nt(qh, kh), a)

    st = st_ref[...]
    o = _dot(a.astype(BF16), v) + _dot_nt((q * jnp.exp(b)).astype(BF16), st.astype(BF16))

    b_last = b[c - 1:c, :]
    kd = (k * jnp.exp(b_last - b)).astype(BF16)
    st_ref[...] = st * jnp.exp(b_last) + _dot_tn(v, kd)

    on = o * lax.rsqrt(jnp.mean(o * o, axis=-1, keepdims=True) + EPS) * ng_ref[...]
    r = r_ref[...].astype(F32) + br_ref[...]
    o_ref[...] = (r * _sigmoid(r) * on).astype(o_ref.dtype)


def _gla_scan(proj, a_low, w_a_up, b_a, b_r, norm_g, bsz, seq):
    t, n3 = proj.shape
    dkk = n3 // 6
    dvv = 2 * dkk
    h = GLA_HEADS
    dk, dv = dkk // h, dvv // h
    c = _pick(seq, GLA_CHUNK)
    nc = seq // c
    voff = (2 * dkk) // dv
    roff = (2 * dkk + dvv) // dv
    rowmap = lambda b, hh, n: b * nc + n
    return pl.pallas_call(
        functools.partial(_gla_scan_kernel, scale=float(dk) ** -0.5),
        grid=(bsz, h, nc),
        in_specs=[pl.BlockSpec((c, dk), lambda b, hh, n: (rowmap(b, hh, n), hh)),
                  pl.BlockSpec((c, dk), lambda b, hh, n: (rowmap(b, hh, n), h + hh)),
                  pl.BlockSpec((c, dv), lambda b, hh, n: (rowmap(b, hh, n), voff + hh)),
                  pl.BlockSpec((c, dv), lambda b, hh, n: (rowmap(b, hh, n), roff + hh)),
                  pl.BlockSpec((c, LANES), lambda b, hh, n: (rowmap(b, hh, n), 0)),
                  pl.BlockSpec((GLA_RANK, dk), lambda b, hh, n: (0, hh)),
                  pl.BlockSpec((1, dk), lambda b, hh, n: (0, hh)),
                  pl.BlockSpec((1, dv), lambda b, hh, n: (0, hh)),
                  pl.BlockSpec((1, dv), lambda b, hh, n: (0, 0)),
                  pl.BlockSpec((c, c), lambda b, hh, n: (0, 0))],
        out_specs=pl.BlockSpec((c, dv), lambda b, hh, n: (rowmap(b, hh, n), hh)),
        out_shape=jax.ShapeDtypeStruct((t, dvv), BF16),
        scratch_shapes=[pltpu.VMEM((dv, dk), F32)],
        compiler_params=_params("parallel", "parallel", "arbitrary"),
        name="gla_scan",
    )(proj, proj, proj, proj, a_low, w_a_up, b_a.reshape(1, dkk), b_r.reshape(1, dvv),
      norm_g.reshape(1, dv), jnp.asarray(_pair_levels(c)))


def _sgu_spatial_kernel(u_ref, v_ref, lg_ref, lb_ref, ws_ref, bs_ref, o_ref, wc_scr):
    rows, half = v_ref.shape
    c = SGU_CHUNK
    gw = half // SGU_GROUPS

    @pl.when(pl.program_id(0) == 0)
    def _():
        row = lax.broadcasted_iota(jnp.int32, (c, c), 0)
        col = lax.broadcasted_iota(jnp.int32, (c, c), 1)
        for g in range(SGU_GROUPS):
            wc_scr[g] = jnp.where(row >= col, ws_ref[g], 0.0).astype(BF16)

    for ci in range(rows // c):
        rs = pl.ds(ci * c, c)
        v = v_ref[rs, :].astype(F32)
        mu = jnp.mean(v, axis=-1, keepdims=True)
        vc = v - mu
        var = jnp.mean(vc * vc, axis=-1, keepdims=True)
        vn = (vc * lax.rsqrt(var + EPS) * lg_ref[...] + lb_ref[...]).astype(BF16)
        for g in range(SGU_GROUPS):
            cs = pl.ds(g * gw, gw)
            mixed = _dot(wc_scr[g], vn[:, g * gw:(g + 1) * gw]) + bs_ref[:, g:g + 1]
            o_ref[rs, cs] = (u_ref[rs, cs].astype(F32) * mixed).astype(o_ref.dtype)


def _sgu_spatial(z, ln_g, ln_b, w_s, b_s, rows=512):
    t, two_half = z.shape
    half = two_half // 2
    c = SGU_CHUNK
    rows = max(c, _pick(t, rows))
    return pl.pallas_call(
        _sgu_spatial_kernel,
        grid=(t // rows,),
        in_specs=[pl.BlockSpec((rows, half), lambda i: (i, 0)),
                  pl.BlockSpec((rows, half), lambda i: (i, 1)),
                  pl.BlockSpec((1, half), lambda i: (0, 0)),
                  pl.BlockSpec((1, half), lambda i: (0, 0)),
                  pl.BlockSpec((SGU_GROUPS, c, c), lambda i: (0, 0, 0)),
                  pl.BlockSpec((c, SGU_GROUPS), lambda i: (0, 0))],
        out_specs=pl.BlockSpec((rows, half), lambda i: (i, 0)),
        out_shape=jax.ShapeDtypeStruct((t, half), BF16),
        scratch_shapes=[pltpu.VMEM((SGU_GROUPS, c, c), BF16)],
        compiler_params=_params("arbitrary"),
        name="sgu_spatial",
    )(z, z, ln_g.reshape(1, half), ln_b.reshape(1, half), w_s, b_s.T)


def _router_kernel(x_ref, g_ref, sh_ref, sc_ref, w_ref, b_ref, h_ref, id_ref, gt_ref, cnt_ref, cnt_scr):
    @pl.when(pl.program_id(0) == 0)
    def _():
        cnt_scr[...] = jnp.zeros_like(cnt_scr)

    h = _norm_mod(x_ref[...], g_ref[...], sh_ref[0], sc_ref[0])
    hi, mid, _ = _split3(h)
    h_ref[...] = hi.reshape(h_ref.shape)
    w = w_ref[...]
    l2 = _dot(hi, w) + _dot(mid, w)
    logits = l2[:, :LANES] + l2[:, LANES:] + b_ref[...]
    lane = lax.broadcasted_iota(jnp.int32, logits.shape, 1)
    lane_f = lane.astype(F32)
    neg = -jnp.inf
    big = float(LANES)

    grp = jnp.where(lane < MOE_GROUPS, logits, neg)
    gmax = jnp.max(grp, axis=-1, keepdims=True)
    gidx = jnp.min(jnp.where(grp == gmax, lane_f, big), axis=-1, keepdims=True)
    g_w = 1.0 / jnp.sum(jnp.exp(grp - gmax), axis=-1, keepdims=True)

    e_lane = lane - MOE_GROUPS
    in_grp = (e_lane >= 0) & (e_lane < MOE_EXPERTS) & ((e_lane // MOE_PER_GROUP).astype(F32) == gidx)
    el = jnp.where(in_grp, logits, neg)
    m1 = jnp.max(el, axis=-1, keepdims=True)
    i1 = jnp.min(jnp.where(el == m1, lane_f, big), axis=-1, keepdims=True)
    el2 = jnp.where(lane_f == i1, neg, el)
    m2 = jnp.max(el2, axis=-1, keepdims=True)
    i2 = jnp.min(jnp.where(el2 == m2, lane_f, big), axis=-1, keepdims=True)
    z = jnp.sum(jnp.exp(el - m1), axis=-1, keepdims=True)
    p1 = 1.0 / z
    p2 = jnp.exp(m2 - m1) / z
    psum = p1 + p2
    e1 = i1 - float(MOE_GROUPS)
    e2 = i2 - float(MOE_GROUPS)

    tm = logits.shape[0]
    oh1 = (lane_f == e1).astype(F32)
    oh2 = (lane_f == e2).astype(F32)
    oh = oh1 + oh2
    row = lax.broadcasted_iota(jnp.int32, (tm, tm), 0)
    col = lax.broadcasted_iota(jnp.int32, (tm, tm), 1)
    before = _dot((row > col).astype(BF16), oh.astype(BF16)) + cnt_scr[...]
    r1 = jnp.sum(oh1 * before, axis=-1, keepdims=True)
    r2 = jnp.sum(oh2 * before, axis=-1, keepdims=True)
    cnt = cnt_scr[...] + jnp.sum(oh, axis=0, keepdims=True)
    cnt_scr[...] = cnt
    cnt_ref[...] = jnp.broadcast_to(cnt, cnt_ref.shape).astype(jnp.int32)

    meta = jnp.where(lane == 0, e1, jnp.where(lane == 1, e2, jnp.where(lane == 2, r1, jnp.where(lane == 3, r2, 0.0))))
    id_ref[...] = meta.astype(jnp.int32)
    gt_ref[...] = jnp.where(lane == 0, g_w * (p1 / psum), jnp.where(lane == 1, g_w * (p2 / psum), 0.0))


def _router(x, gain, shift, scale, w_grp, b_grp, w_exp, b_exp, rows_per_batch, tm=512):
    t, d = x.shape
    bsz = shift.shape[0]
    tm = _pick(rows_per_batch, tm)
    tpb = rows_per_batch // tm
    n_log = MOE_GROUPS + MOE_EXPERTS
    wcat = jnp.zeros((d, LANES), F32).at[:, :MOE_GROUPS].set(w_grp).at[:, MOE_GROUPS:n_log].set(w_exp)
    w_hi = wcat.astype(BF16)
    w_lo = (wcat - w_hi.astype(F32)).astype(BF16)
    w2 = jnp.concatenate([w_hi, w_lo], axis=1)
    bcat = jnp.zeros((1, LANES), F32).at[0, :MOE_GROUPS].set(b_grp).at[0, MOE_GROUPS:n_log].set(b_exp)
    return pl.pallas_call(
        _router_kernel,
        grid=(t // tm,),
        in_specs=[pl.BlockSpec((tm, d), lambda i: (i, 0)),
                  pl.BlockSpec((1, d), lambda i: (0, 0)),
                  pl.BlockSpec((1, 1, d), lambda i: (i // tpb, 0, 0)),
                  pl.BlockSpec((1, 1, d), lambda i: (i // tpb, 0, 0)),
                  pl.BlockSpec((d, 2 * LANES), lambda i: (0, 0)),
                  pl.BlockSpec((1, LANES), lambda i: (0, 0))],
        out_specs=[pl.BlockSpec((tm, d // LANES, LANES), lambda i: (i, 0, 0)),
                   pl.BlockSpec((tm, LANES), lambda i: (i, 0)),
                   pl.BlockSpec((tm, LANES), lambda i: (i, 0)),
                   pl.BlockSpec((8, LANES), lambda i: (0, 0))],
        out_shape=[jax.ShapeDtypeStruct((t, d // LANES, LANES), BF16),
                   jax.ShapeDtypeStruct((t, LANES), jnp.int32),
                   jax.ShapeDtypeStruct((t, LANES), F32),
                   jax.ShapeDtypeStruct((8, LANES), jnp.int32)],
        scratch_shapes=[pltpu.VMEM((1, LANES), F32)],
        compiler_params=_params("arbitrary"),
        name="moe_router",
    )(x, gain.reshape(1, d), shift.reshape(bsz, 1, d), scale.reshape(bsz, 1, d), w2, bcat)


def _dispatch_tables(ids, rank, counts, n_tiles, tile):
    blk = MOE_BLOCK
    a = ids.shape[0] * MOE_TOPK
    padded = (counts + blk - 1) // blk * blk
    pad_ends = jnp.cumsum(padded)
    pad_starts = pad_ends - padded
    onehot = ids[:, :, None] == jnp.arange(MOE_EXPERTS, dtype=jnp.int32)
    dest = jnp.sum(jnp.where(onehot, pad_starts, 0), axis=-1) + rank
    dest_tiles = dest.reshape(n_tiles, tile, MOE_TOPK).transpose(0, 2, 1).reshape(n_tiles, 1, MOE_TOPK * tile)
    nb = -(-(a + MOE_EXPERTS * (blk - 1)) // blk)
    block_start = jnp.arange(nb, dtype=jnp.int32) * blk
    block_exp = jnp.minimum(jnp.sum(pad_ends[None, :] <= block_start[:, None], axis=1), MOE_EXPERTS - 1)
    n_used = (pad_ends[-1] // blk).reshape(1)
    fill = jnp.stack([pad_starts + counts, padded - counts,
                      jnp.broadcast_to(n_used, (MOE_EXPERTS,))])
    first = jnp.concatenate([jnp.ones((1,), bool), block_exp[1:] != block_exp[:-1]])
    slot = (jnp.cumsum(first) - 1) % 2
    e_ids = jnp.arange(MOE_EXPERTS, dtype=jnp.int32)
    later = jnp.where((counts > 0)[None, :] & (e_ids[None, :] > e_ids[:, None]), e_ids[None, :], MOE_EXPERTS)
    nxt = jnp.min(later, axis=1)
    nxt = jnp.where(nxt < MOE_EXPERTS, nxt, -1)[block_exp]
    sched = jnp.stack([block_exp, first.astype(jnp.int32), slot, nxt])
    return dest_tiles.astype(jnp.int32), sched.astype(jnp.int32), n_used.astype(jnp.int32), fill.astype(jnp.int32)


def _dispatch_kernel(fill_ref, dest_ref, h_ref, xs_hbm, zbuf, sem, zsem):
    tm = h_ref.shape[0]
    i = pl.program_id(0)

    def fill_copies(start):
        def issue(cp, on):
            @pl.when(on)
            def _():
                cp.start() if start else cp.wait()

        for e in range(MOE_EXPERTS):
            off = fill_ref[0, e]
            n_pad = fill_ref[1, e]
            bit = MOE_BLOCK // 2
            while bit:
                issue(pltpu.make_async_copy(zbuf.at[pl.ds(0, bit)], xs_hbm.at[pl.ds(off, bit)], zsem),
                      (n_pad & bit) != 0)
                off = off + (n_pad & bit)
                bit //= 2

        half = zbuf.shape[0]
        first = fill_ref[2, 0] * (MOE_BLOCK // half)

        def tail(j, c):
            cp = pltpu.make_async_copy(zbuf, xs_hbm.at[pl.ds(j * half, half)], zsem)
            cp.start() if start else cp.wait()
            return c
        lax.fori_loop(first, xs_hbm.shape[0] // half, tail, 0)

    @pl.when(i == 0)
    def _():
        zbuf[...] = jnp.zeros_like(zbuf)
        fill_copies(True)
        fill_copies(False)

    def row_copy(j, dst):
        return pltpu.make_async_copy(h_ref.at[pl.ds(j % tm, 1)], xs_hbm.at[pl.ds(dst, 1)], sem)

    for j in range(MOE_TOPK * tm):
        row_copy(j, dest_ref[0, 0, j]).start(priority=j % 2)

    def wait(j, c):
        row_copy(0, 0).wait()
        return c
    lax.fori_loop(0, MOE_TOPK * tm, wait, 0, unroll=8)


def _dispatch(h2, dest_tiles, fill, n_rows):
    t, sub, lanes = h2.shape
    nt, _, two_tm = dest_tiles.shape
    tm = two_tm // MOE_TOPK
    grid_spec = pltpu.PrefetchScalarGridSpec(
        num_scalar_prefetch=1,
        grid=(nt,),
        in_specs=[pl.BlockSpec((1, 1, two_tm), lambda i, f: (i, 0, 0), memory_space=pltpu.SMEM),
                  pl.BlockSpec((tm, sub, lanes), lambda i, f: (i, 0, 0))],
        out_specs=pl.BlockSpec(memory_space=pl.ANY),
        scratch_shapes=[pltpu.VMEM((MOE_BLOCK // 2, sub, lanes), BF16),
                        pltpu.SemaphoreType.DMA(()),
                        pltpu.SemaphoreType.DMA(())],
    )
    return pl.pallas_call(
        _dispatch_kernel,
        grid_spec=grid_spec,
        out_shape=jax.ShapeDtypeStruct((n_rows, sub, lanes), BF16),
        compiler_params=_params("arbitrary"),
        name="moe_dispatch",
    )(fill, dest_tiles, h2)


def _experts_kernel(sched_ref, nused_ref, x_ref, w1_hbm, w3_hbm, w2_hbm, y_ref,
                    w1f, w3f, w2f, w1b, w3b, w2b, wsem, *, layer):
    i = pl.program_id(0)
    used = i < nused_ref[0]
    e, first, slot, e_next = (sched_ref[r, i] for r in range(4))

    def weight_copies(ex, s):
        return [pltpu.make_async_copy(src.at[layer, ex], dst.at[s], wsem.at[s])
                for src, dst in ((w1_hbm, w1f), (w3_hbm, w3f), (w2_hbm, w2f))]

    @pl.when(i == 0)
    def _():
        for cp in weight_copies(e, 0):
            cp.start()

    @pl.when(used & (first == 1))
    def _():
        for cp in weight_copies(e, slot):
            cp.wait()

        @pl.when(e_next >= 0)
        def _():
            for cp in weight_copies(e_next, 1 - slot):
                cp.start()

        w1b[...] = w1f[slot].astype(BF16)
        w3b[...] = w3f[slot].astype(BF16)
        w2b[...] = w2f[slot].astype(BF16)

    @pl.when(used)
    def _():
        blk, sub, lanes = x_ref.shape
        x = x_ref[...].reshape(blk, sub * lanes)
        a = _dot(x, w1b[...])
        hb = (a * _sigmoid(a)) * _dot(x, w3b[...])
        y_ref[...] = _dot(hb.astype(BF16), w2b[...]).astype(BF16).reshape(blk, sub, lanes)

    @pl.when(jnp.logical_not(used))
    def _():
        y_ref[...] = jnp.zeros_like(y_ref)


def _experts(xs, sched, n_used, w1, w3, w2, layer):
    nb = sched.shape[1]
    _, sub, lanes = xs.shape
    d = sub * lanes
    hid = w1.shape[-1]
    blk = MOE_BLOCK

    def xmap(i, sc, nu):
        return (jnp.minimum(i, jnp.maximum(nu[0], 1) - 1), 0, 0)

    grid_spec = pltpu.PrefetchScalarGridSpec(
        num_scalar_prefetch=2,
        grid=(nb,),
        in_specs=[pl.BlockSpec((blk, sub, lanes), xmap),
                  pl.BlockSpec(memory_space=pl.ANY),
                  pl.BlockSpec(memory_space=pl.ANY),
                  pl.BlockSpec(memory_space=pl.ANY)],
        out_specs=pl.BlockSpec((blk, sub, lanes), lambda i, sc, nu: (i, 0, 0)),
        scratch_shapes=[pltpu.VMEM((2, d, hid), F32),
                        pltpu.VMEM((2, d, hid), F32),
                        pltpu.VMEM((2, hid, d), F32),
                        pltpu.VMEM((d, hid), BF16),
                        pltpu.VMEM((d, hid), BF16),
                        pltpu.VMEM((hid, d), BF16),
                        pltpu.SemaphoreType.DMA((2,))],
    )
    return pl.pallas_call(
        functools.partial(_experts_kernel, layer=layer),
        grid_spec=grid_spec,
        out_shape=jax.ShapeDtypeStruct((nb * blk, sub, lanes), BF16),
        compiler_params=_params("arbitrary"),
        name="moe_experts",
    )(sched, n_used, xs, w1, w3, w2)


def _combine_kernel(dc_ref, dn_ref, x_ref, gt_ref, gate_ref, fn_ref, ys_hbm, o_ref, ybuf, sem, *, final):
    tm = x_ref.shape[0]
    n_rows = MOE_TOPK * tm
    i = pl.program_id(0)
    nt = pl.num_programs(0)
    slot = lax.rem(i, 2)

    def row_copy(row, j, s):
        return pltpu.make_async_copy(ys_hbm.at[pl.ds(row, 1)], ybuf.at[s, pl.ds(j, 1)], sem.at[s])

    def start_all(dst_ref, s):
        for j in range(n_rows):
            row_copy(dst_ref[0, 0, j], j, s).start(priority=j % 2)

    @pl.when(i == 0)
    def _():
        start_all(dc_ref, 0)

    def wait_body(j, c):
        row_copy(0, 0, slot).wait()
        return c
    lax.fori_loop(0, n_rows, wait_body, 0, unroll=8)

    for s in range(2):
        @pl.when((i + 1 < nt) & (slot == 1 - s))
        def _(s=s):
            start_all(dn_ref, s)

    gt = gt_ref[...]
    d = x_ref.shape[1]
    y0 = ybuf[slot, pl.ds(0, tm)].reshape(tm, d).astype(F32)
    y1 = ybuf[slot, pl.ds(tm, tm)].reshape(tm, d).astype(F32)
    x = x_ref[...] + gate_ref[0] * (gt[:, 0:1] * y0 + gt[:, 1:2] * y1)
    if final:
        x = x * lax.rsqrt(jnp.mean(x * x, axis=-1, keepdims=True) + EPS) * fn_ref[...]
    o_ref[...] = x


def _combine(x, ys, dest_tiles, gates, gate, final_gain, rows_per_batch, final):
    t, d = x.shape
    bsz = gate.shape[0]
    nt, _, two_tm = dest_tiles.shape
    tm = two_tm // MOE_TOPK
    tpb = rows_per_batch // tm
    smem = functools.partial(pl.BlockSpec, memory_space=pltpu.SMEM)
    return pl.pallas_call(
        functools.partial(_combine_kernel, final=final),
        grid=(nt,),
        in_specs=[smem((1, 1, two_tm), lambda i: (i, 0, 0)),
                  smem((1, 1, two_tm), lambda i: (jnp.minimum(i + 1, nt - 1), 0, 0)),
                  pl.BlockSpec((tm, d), lambda i: (i, 0)),
                  pl.BlockSpec((tm, LANES), lambda i: (i, 0)),
                  pl.BlockSpec((1, 1, d), lambda i: (i // tpb, 0, 0)),
                  pl.BlockSpec((1, d), lambda i: (0, 0)),
                  pl.BlockSpec(memory_space=pl.ANY)],
        out_specs=pl.BlockSpec((tm, d), lambda i: (i, 0)),
        out_shape=jax.ShapeDtypeStruct((t, d), F32),
        scratch_shapes=[pltpu.VMEM((2, two_tm) + ys.shape[1:], BF16),
                        pltpu.SemaphoreType.DMA((2,))],
        compiler_params=_params("arbitrary"),
        name="moe_combine",
    )(dest_tiles, dest_tiles, x, gates, gate.reshape(bsz, 1, d), final_gain.reshape(1, d), ys)


def _moe(x, gain, shift, scale, gate, w_grp, b_grp, w_exp, b_exp, w1, w3, w2, layer, final_gain, seq, final):
    t, _ = x.shape
    tile = _pick(seq, MOE_TOKEN_TILE)
    h2, meta, gts, cnt = _router(x, gain, shift, scale, w_grp, b_grp, w_exp, b_exp, seq)
    dest_tiles, sched, n_used, fill = _dispatch_tables(
        meta[:, :MOE_TOPK], meta[:, MOE_TOPK:2 * MOE_TOPK], cnt[0, :MOE_EXPERTS], t // tile, tile)
    xs = _dispatch(h2, dest_tiles, fill, sched.shape[1] * MOE_BLOCK)
    ys = _experts(xs, sched, n_used, w1, w3, w2, layer)
    return _combine(x, ys, dest_tiles, gts, gate, final_gain, seq, final)


def kernel(x, c, norm_mix, norm_ffn, ada_w, ada_b, gla_w_in, gla_b_r, gla_w_a_up, gla_b_a, gla_norm_g, gla_w_out, sgu_w_in, sgu_b_in, sgu_ln_g, sgu_ln_b, sgu_w_s, sgu_b_s, sgu_w_out, moe_w_grp, moe_b_grp, moe_w_exp, moe_b_exp, moe_w1, moe_w3, moe_w2, final_norm):
    bsz, seq, d = x.shape
    depth = norm_mix.shape[0]
    xt = x.reshape(bsz * seq, d)
    mod = _adaln(c, ada_w, ada_b)
    for i in range(depth):
        sh1, sc1, g1, sh2, sc2, g2 = [mod[i, :, m * d:(m + 1) * d] for m in range(N_MOD)]
        j = i // 2
        if i % 2 == 0:
            w_in = gla_w_in[j]
            n_main = w_in.shape[1] - GLA_RANK
            w_main = w_in[:, :n_main].astype(BF16)
            w_low = jnp.zeros((d, LANES), F32).at[:, :GLA_RANK].set(w_in[:, n_main:]).astype(BF16)
            proj, a_low = _norm_matmul(xt, norm_mix[i], sh1, sc1, w_main, seq, "gla_in_proj", w_side=w_low)
            o = _gla_scan(proj, a_low, gla_w_a_up[j], gla_b_a[j], gla_b_r[j], gla_norm_g[j], bsz, seq)
            xt = _matmul_residual(o, gla_w_out[j].astype(BF16), xt, g1, seq, "gla_out_proj")
        else:
            z, = _norm_matmul(xt, norm_mix[i], sh1, sc1, sgu_w_in[j].astype(BF16), seq, "sgu_in_proj",
                              bias=sgu_b_in[j])
            o = _sgu_spatial(z, sgu_ln_g[j], sgu_ln_b[j], sgu_w_s[j], sgu_b_s[j])
            xt = _matmul_residual(o, sgu_w_out[j].astype(BF16), xt, g1, seq, "sgu_out_proj", tm=512)
        xt = _moe(xt, norm_ffn[i], sh2, sc2, g2, moe_w_grp[i], moe_b_grp[i], moe_w_exp[i], moe_b_exp[i],
                  moe_w1, moe_w3, moe_w2, i, final_norm, seq, final=(i == depth - 1))
    return xt.reshape(bsz, seq, d)
```

```python
import functools
import math

import jax
import jax.numpy as jnp
import numpy as np
from jax import lax
from jax.experimental import pallas as pl
from jax.experimental.pallas import tpu as pltpu

F32 = jnp.float32
BF16 = jnp.bfloat16

EPS = 1e-6
N_MOD = 6
GLA_HEADS = 4
GLA_RANK = 16
GLA_TAU = 16.0
GLA_CHUNK = 256
GLA_FLAT = 8
SGU_CHUNK = 128
SGU_GROUPS = 8
MOE_GROUPS = 8
MOE_PER_GROUP = 8
MOE_EXPERTS = MOE_GROUPS * MOE_PER_GROUP
MOE_TOPK = 2
MOE_BLOCK = 256
MOE_TOKEN_TILE = 256
COMBINE_PIECE_ROWS = 16
LANES = 128
V7X_VMEM_LIMIT_BYTES = 56 * 2**20


def _params(*semantics):
    return pltpu.CompilerParams(dimension_semantics=semantics,
                                vmem_limit_bytes=V7X_VMEM_LIMIT_BYTES)


def _dot(a, b):
    return jnp.dot(a, b, preferred_element_type=F32)


def _dot_nt(a, b):
    return lax.dot_general(a, b, (((1,), (1,)), ((), ())), preferred_element_type=F32)


def _dot_tn(a, b):
    return lax.dot_general(a, b, (((0,), (0,)), ((), ())), preferred_element_type=F32)


def _split3(x):
    hi = x.astype(BF16)
    r1 = x - hi.astype(F32)
    mid = r1.astype(BF16)
    lo = (r1 - mid.astype(F32)).astype(BF16)
    return hi, mid, lo


def _sigmoid(x):
    return 1.0 / (1.0 + jnp.exp(-x))


def _gelu_tanh(x):
    c = math.sqrt(2.0 / math.pi)
    return x * (0.5 * (1.0 + jnp.tanh(c * (x + 0.044715 * (x * x * x)))))


def _pick(n, pref):
    t = min(n, pref)
    while n % t:
        t //= 2
    return t


def _adaln_kernel(c_ref, w_ref, b_ref, o_ref):
    c = c_ref[...]
    ca = c * _sigmoid(c)
    hi, mid, _ = _split3(ca)
    w = w_ref[0].astype(BF16)
    o_ref[0] = _dot(hi, w) + _dot(mid, w) + b_ref[0]


def _adaln(c, ada_w, ada_b):
    depth, d, n = ada_w.shape
    b = c.shape[0]
    rows = 8
    cp = jnp.zeros((rows, d), F32).at[:b].set(c)
    tn = _pick(n, 1024)
    out = pl.pallas_call(
        _adaln_kernel,
        grid=(depth, n // tn),
        in_specs=[pl.BlockSpec((rows, d), lambda l, j: (0, 0)),
                  pl.BlockSpec((1, d, tn), lambda l, j: (l, 0, j)),
                  pl.BlockSpec((1, 1, tn), lambda l, j: (l, 0, j))],
        out_specs=pl.BlockSpec((1, rows, tn), lambda l, j: (l, 0, j)),
        out_shape=jax.ShapeDtypeStruct((depth, rows, n), F32),
        compiler_params=_params("parallel", "parallel"),
        name="adaln",
    )(cp, ada_w, ada_b.reshape(depth, 1, n))
    return out[:, :b]


def _norm_mod(x, gain, shift, scale):
    ms = jnp.mean(x * x, axis=-1, keepdims=True)
    return (x * lax.rsqrt(ms + EPS) * gain) * (1.0 + scale) + shift


def _norm_matmul_kernel(x_ref, g_ref, sh_ref, sc_ref, w_ref, *rest, gelu, side, normed):
    rest = list(rest)
    bias_ref = rest.pop(0) if gelu else None
    wside_ref = rest.pop(0) if side else None
    o_ref = rest.pop(0)
    oside_ref = rest.pop(0) if side else None
    h_ref = x_ref if normed else rest.pop(0)

    if not normed or side:
        @pl.when(pl.program_id(1) == 0)
        def _():
            if not normed:
                h_ref[...] = _norm_mod(x_ref[...], g_ref[...], sh_ref[0], sc_ref[0]).astype(BF16)
            if side:
                oside_ref[...] = _dot(h_ref[...], wside_ref[...])

    acc = _dot(h_ref[...], w_ref[...])
    if gelu:
        acc = _gelu_tanh(acc + bias_ref[...])
    o_ref[...] = acc.astype(o_ref.dtype)


def _norm_matmul(x, gain, shift, scale, w, rows_per_batch, name, bias=None, w_side=None, tm=1024, tn=1024):
    normed = x.dtype == BF16
    t, d = x.shape
    n = w.shape[1]
    tm = _pick(rows_per_batch, tm)
    tn = _pick(n, tn)
    tpb = rows_per_batch // tm
    bsz = shift.shape[0]
    in_specs = [pl.BlockSpec((tm, d), lambda i, j: (i, 0)),
                pl.BlockSpec((1, d), lambda i, j: (0, 0)),
                pl.BlockSpec((1, 1, d), lambda i, j: (i // tpb, 0, 0)),
                pl.BlockSpec((1, 1, d), lambda i, j: (i // tpb, 0, 0)),
                pl.BlockSpec((d, tn), lambda i, j: (0, j))]
    args = [x, gain.reshape(1, d), shift.reshape(bsz, 1, d), scale.reshape(bsz, 1, d), w]
    out_specs = [pl.BlockSpec((tm, tn), lambda i, j: (i, j))]
    out_shape = [jax.ShapeDtypeStruct((t, n), BF16)]
    if bias is not None:
        in_specs.append(pl.BlockSpec((1, tn), lambda i, j: (0, j)))
        args.append(bias.reshape(1, n))
    if w_side is not None:
        in_specs.append(pl.BlockSpec((d, LANES), lambda i, j: (0, 0)))
        args.append(w_side)
        out_specs.append(pl.BlockSpec((tm, LANES), lambda i, j: (i, 0)))
        out_shape.append(jax.ShapeDtypeStruct((t, LANES), F32))
    return pl.pallas_call(
        functools.partial(_norm_matmul_kernel, gelu=bias is not None, side=w_side is not None, normed=normed),
        grid=(t // tm, n // tn),
        in_specs=in_specs,
        out_specs=out_specs,
        out_shape=out_shape,
        scratch_shapes=[] if normed else [pltpu.VMEM((tm, d), BF16)],
        compiler_params=_params("parallel", "arbitrary"),
        name=name,
    )(*args)


def _matmul_residual_kernel(a_ref, w_ref, x_ref, gate_ref, o_ref):
    o_ref[...] = x_ref[...] + gate_ref[0] * _dot(a_ref[...], w_ref[...])


def _matmul_residual(a, w, x, gate, rows_per_batch, name, tm=512):
    t, k = a.shape
    n = w.shape[1]
    tm = _pick(rows_per_batch, tm)
    tpb = rows_per_batch // tm
    bsz = gate.shape[0]
    return pl.pallas_call(
        _matmul_residual_kernel,
        grid=(t // tm,),
        in_specs=[pl.BlockSpec((tm, k), lambda i: (i, 0)),
                  pl.BlockSpec((k, n), lambda i: (0, 0), pipeline_mode=pl.Buffered(1)),
                  pl.BlockSpec((tm, n), lambda i: (i, 0)),
                  pl.BlockSpec((1, 1, n), lambda i: (i // tpb, 0, 0))],
        out_specs=pl.BlockSpec((tm, n), lambda i: (i, 0)),
        out_shape=jax.ShapeDtypeStruct((t, n), F32),
        compiler_params=_params("parallel"),
        name=name,
    )(a, w, x, gate.reshape(bsz, 1, n))


def _rows_bcast(b, n, off):
    c, dk = b.shape
    parts = [jnp.broadcast_to(b[j * n + off:j * n + off + 1, :], (n, dk)) for j in range(c // n)]
    return parts[0] if len(parts) == 1 else jnp.concatenate(parts, axis=0)


def _pair_levels(c):
    t = np.arange(c)[:, None]
    s = np.arange(c)[None, :]
    x = t ^ s
    lvl = np.floor(np.log2(np.maximum(x, 1))).astype(np.int32) + 1
    lvl = np.where(x < GLA_FLAT, 0, lvl)
    return np.where(s <= t, lvl, -1).astype(np.int32)


def _gla_scan_kernel(q_ref, k_ref, v_ref, r_ref, al_ref, wup_ref, ba_ref, br_ref, ng_ref, lvl_ref,
                     o_ref, st_ref, *, scale):
    c, dk = q_ref.shape

    @pl.when(pl.program_id(2) == 0)
    def _():
        st_ref[...] = jnp.zeros_like(st_ref)

    q = q_ref[...].astype(F32) * scale
    k = k_ref[...].astype(F32)
    v = v_ref[...]

    al = _split3(al_ref[:, :GLA_RANK])
    wu = _split3(wup_ref[...])
    pre = _dot(al[0], wu[0]) + _dot(al[0], wu[1]) + _dot(al[1], wu[0]) + ba_ref[...]
    g = (jnp.minimum(pre, 0.0) - jnp.log(1.0 + jnp.exp(-jnp.abs(pre)))) * (1.0 / GLA_TAU)

    lvl = lvl_ref[...]
    tri = jnp.where(lvl >= 0, 1.0, 0.0).astype(BF16)
    gs = _split3(g)
    b = _dot(tri, gs[0]) + _dot(tri, gs[1]) + _dot(tri, gs[2])

    a = jnp.zeros((c, c), F32)
    n = c
    while n > GLA_FLAT:
        e = jnp.exp(-jnp.abs(b - _rows_bcast(b, n, n // 2)))
        p = _dot_nt((q * e).astype(BF16), (k * e).astype(BF16))
        a = jnp.where(lvl == n.bit_length() - 1, p, a)
        n //= 2
    ref = _rows_bcast(b, GLA_FLAT, 0)
    qh = (q * jnp.exp(b - ref)).astype(BF16)
    kh = (k * jnp.exp(ref - b)).astype(BF16)
    a = jnp.where(lvl == 0, _dot_nt(qh, kh), a)

    st = st_ref[...]
    o = _dot(a.astype(BF16), v) + _dot_nt((q * jnp.exp(b)).astype(BF16), st.astype(BF16))

    b_last = b[c - 1:c, :]
    kd = (k * jnp.exp(b_last - b)).astype(BF16)
    st_ref[...] = st * jnp.exp(b_last) + _dot_tn(v, kd)

    on = o * lax.rsqrt(jnp.mean(o * o, axis=-1, keepdims=True) + EPS) * ng_ref[...]
    r = r_ref[...].astype(F32) + br_ref[...]
    o_ref[...] = (r * _sigmoid(r) * on).astype(o_ref.dtype)


def _gla_scan(proj, a_low, w_a_up, b_a, b_r, norm_g, bsz, seq):
    t, n3 = proj.shape
    dkk = n3 // 6
    dvv = 2 * dkk
    h = GLA_HEADS
    dk, dv = dkk // h, dvv // h
    c = _pick(seq, GLA_CHUNK)
    nc = seq // c
    voff = (2 * dkk) // dv
    roff = (2 * dkk + dvv) // dv
    rowmap = lambda b, hh, n: b * nc + n
    return pl.pallas_call(
        functools.partial(_gla_scan_kernel, scale=float(dk) ** -0.5),
        grid=(bsz, h, nc),
        in_specs=[pl.BlockSpec((c, dk), lambda b, hh, n: (rowmap(b, hh, n), hh)),
                  pl.BlockSpec((c, dk), lambda b, hh, n: (rowmap(b, hh, n), h + hh)),
                  pl.BlockSpec((c, dv), lambda b, hh, n: (rowmap(b, hh, n), voff + hh)),
                  pl.BlockSpec((c, dv), lambda b, hh, n: (rowmap(b, hh, n), roff + hh)),
                  pl.BlockSpec((c, LANES), lambda b, hh, n: (rowmap(b, hh, n), 0)),
                  pl.BlockSpec((GLA_RANK, dk), lambda b, hh, n: (0, hh)),
                  pl.BlockSpec((1, dk), lambda b, hh, n: (0, hh)),
                  pl.BlockSpec((1, dv), lambda b, hh, n: (0, hh)),
                  pl.BlockSpec((1, dv), lambda b, hh, n: (0, 0)),
                  pl.BlockSpec((c, c), lambda b, hh, n: (0, 0))],
        out_specs=pl.BlockSpec((c, dv), lambda b, hh, n: (rowmap(b, hh, n), hh)),
        out_shape=jax.ShapeDtypeStruct((t, dvv), BF16),
        scratch_shapes=[pltpu.VMEM((dv, dk), F32)],
        compiler_params=_params("parallel", "parallel", "arbitrary"),
        name="gla_scan",
    )(proj, proj, proj, proj, a_low, w_a_up, b_a.reshape(1, dkk), b_r.reshape(1, dvv),
      norm_g.reshape(1, dv), jnp.asarray(_pair_levels(c)))


def _sgu_spatial_kernel(u_ref, v_ref, lg_ref, lb_ref, ws_ref, bs_ref, o_ref, wc_scr):
    rows, half = v_ref.shape
    c = SGU_CHUNK
    gw = half // SGU_GROUPS

    @pl.when(pl.program_id(0) == 0)
    def _():
        row = lax.broadcasted_iota(jnp.int32, (c, c), 0)
        col = lax.broadcasted_iota(jnp.int32, (c, c), 1)
        for g in range(SGU_GROUPS):
            wc_scr[g] = jnp.where(row >= col, ws_ref[g], 0.0).astype(BF16)

    for ci in range(rows // c):
        rs = pl.ds(ci * c, c)
        v = v_ref[rs, :].astype(F32)
        mu = jnp.mean(v, axis=-1, keepdims=True)
        vc = v - mu
        var = jnp.mean(vc * vc, axis=-1, keepdims=True)
        vn = (vc * lax.rsqrt(var + EPS) * lg_ref[...] + lb_ref[...]).astype(BF16)
        for g in range(SGU_GROUPS):
            cs = pl.ds(g * gw, gw)
            mixed = _dot(wc_scr[g], vn[:, g * gw:(g + 1) * gw]) + bs_ref[:, g:g + 1]
            o_ref[rs, cs] = (u_ref[rs, cs].astype(F32) * mixed).astype(o_ref.dtype)


def _sgu_spatial(z, ln_g, ln_b, w_s, b_s, rows=512):
    t, two_half = z.shape
    half = two_half // 2
    c = SGU_CHUNK
    rows = max(c, _pick(t, rows))
    return pl.pallas_call(
        _sgu_spatial_kernel,
        grid=(t // rows,),
        in_specs=[pl.BlockSpec((rows, half), lambda i: (i, 0)),
                  pl.BlockSpec((rows, half), lambda i: (i, 1)),
                  pl.BlockSpec((1, half), lambda i: (0, 0)),
                  pl.BlockSpec((1, half), lambda i: (0, 0)),
                  pl.BlockSpec((SGU_GROUPS, c, c), lambda i: (0, 0, 0)),
                  pl.BlockSpec((c, SGU_GROUPS), lambda i: (0, 0))],
        out_specs=pl.BlockSpec((rows, half), lambda i: (i, 0)),
        out_shape=jax.ShapeDtypeStruct((t, half), BF16),
        scratch_shapes=[pltpu.VMEM((SGU_GROUPS, c, c), BF16)],
        compiler_params=_params("arbitrary"),
        name="sgu_spatial",
    )(z, z, ln_g.reshape(1, half), ln_b.reshape(1, half), w_s, b_s.T)


def _route(h, w, bias, cnt):
    hi, mid, _ = _split3(h)
    l2 = _dot(hi, w) + _dot(mid, w)
    logits = l2[:, :LANES] + l2[:, LANES:] + bias
    lane = lax.broadcasted_iota(jnp.int32, logits.shape, 1)
    lane_f = lane.astype(F32)
    neg = -jnp.inf
    big = float(LANES)

    grp = jnp.where(lane < MOE_GROUPS, logits, neg)
    gmax = jnp.max(grp, axis=-1, keepdims=True)
    gidx = jnp.min(jnp.where(grp == gmax, lane_f, big), axis=-1, keepdims=True)
    g_w = 1.0 / jnp.sum(jnp.exp(grp - gmax), axis=-1, keepdims=True)

    e_lane = lane - MOE_GROUPS
    in_grp = (e_lane >= 0) & (e_lane < MOE_EXPERTS) & ((e_lane // MOE_PER_GROUP).astype(F32) == gidx)
    el = jnp.where(in_grp, logits, neg)
    m1 = jnp.max(el, axis=-1, keepdims=True)
    i1 = jnp.min(jnp.where(el == m1, lane_f, big), axis=-1, keepdims=True)
    el2 = jnp.where(lane_f == i1, neg, el)
    m2 = jnp.max(el2, axis=-1, keepdims=True)
    i2 = jnp.min(jnp.where(el2 == m2, lane_f, big), axis=-1, keepdims=True)
    z = jnp.sum(jnp.exp(el - m1), axis=-1, keepdims=True)
    p1 = 1.0 / z
    p2 = jnp.exp(m2 - m1) / z
    psum = p1 + p2
    e1 = i1 - float(MOE_GROUPS)
    e2 = i2 - float(MOE_GROUPS)

    tm = logits.shape[0]
    oh1 = (lane_f == e1).astype(F32)
    oh2 = (lane_f == e2).astype(F32)
    oh = oh1 + oh2
    row = lax.broadcasted_iota(jnp.int32, (tm, tm), 0)
    col = lax.broadcasted_iota(jnp.int32, (tm, tm), 1)
    before = _dot((row > col).astype(BF16), oh.astype(BF16)) + cnt
    r1 = jnp.sum(oh1 * before, axis=-1, keepdims=True)
    r2 = jnp.sum(oh2 * before, axis=-1, keepdims=True)
    cnt = cnt + jnp.sum(oh, axis=0, keepdims=True)

    meta = jnp.where(lane == 0, e1, jnp.where(lane == 1, e2, jnp.where(lane == 2, r1, jnp.where(lane == 3, r2, 0.0))))
    gates = jnp.where(lane == 0, g_w * (p1 / psum), jnp.where(lane == 1, g_w * (p2 / psum), 0.0))
    return hi, meta.astype(jnp.int32), gates, cnt


def _router_kernel(x_ref, g_ref, sh_ref, sc_ref, w_ref, b_ref, h_ref, id_ref, gt_ref, cnt_ref, cnt_scr):
    @pl.when(pl.program_id(0) == 0)
    def _():
        cnt_scr[...] = jnp.zeros_like(cnt_scr)

    h = _norm_mod(x_ref[...], g_ref[...], sh_ref[0], sc_ref[0])
    hi, meta, gates, cnt = _route(h, w_ref[...], b_ref[...], cnt_scr[...])
    h_ref[...] = hi.reshape(h_ref.shape)
    id_ref[...] = meta
    gt_ref[...] = gates
    cnt_scr[...] = cnt
    cnt_ref[...] = jnp.broadcast_to(cnt, cnt_ref.shape).astype(jnp.int32)


def _router(x, gain, shift, scale, w_grp, b_grp, w_exp, b_exp, rows_per_batch, tm=512):
    t, d = x.shape
    bsz = shift.shape[0]
    tm = _pick(rows_per_batch, tm)
    tpb = rows_per_batch // tm
    n_log = MOE_GROUPS + MOE_EXPERTS
    wcat = jnp.zeros((d, LANES), F32).at[:, :MOE_GROUPS].set(w_grp).at[:, MOE_GROUPS:n_log].set(w_exp)
    w_hi = wcat.astype(BF16)
    w_lo = (wcat - w_hi.astype(F32)).astype(BF16)
    w2 = jnp.concatenate([w_hi, w_lo], axis=1)
    bcat = jnp.zeros((1, LANES), F32).at[0, :MOE_GROUPS].set(b_grp).at[0, MOE_GROUPS:n_log].set(b_exp)
    return pl.pallas_call(
        _router_kernel,
        grid=(t // tm,),
        in_specs=[pl.BlockSpec((tm, d), lambda i: (i, 0)),
                  pl.BlockSpec((1, d), lambda i: (0, 0)),
                  pl.BlockSpec((1, 1, d), lambda i: (i // tpb, 0, 0)),
                  pl.BlockSpec((1, 1, d), lambda i: (i // tpb, 0, 0)),
                  pl.BlockSpec((d, 2 * LANES), lambda i: (0, 0)),
                  pl.BlockSpec((1, LANES), lambda i: (0, 0))],
        out_specs=[pl.BlockSpec((tm, d // LANES, LANES), lambda i: (i, 0, 0)),
                   pl.BlockSpec((tm, LANES), lambda i: (i, 0)),
                   pl.BlockSpec((tm, LANES), lambda i: (i, 0)),
                   pl.BlockSpec((8, LANES), lambda i: (0, 0))],
        out_shape=[jax.ShapeDtypeStruct((t, d // LANES, LANES), BF16),
                   jax.ShapeDtypeStruct((t, LANES), jnp.int32),
                   jax.ShapeDtypeStruct((t, LANES), F32),
                   jax.ShapeDtypeStruct((8, LANES), jnp.int32)],
        scratch_shapes=[pltpu.VMEM((1, LANES), F32)],
        compiler_params=_params("arbitrary"),
        name="moe_router",
    )(x, gain.reshape(1, d), shift.reshape(bsz, 1, d), scale.reshape(bsz, 1, d), w2, bcat)


def _dispatch_tables(ids, rank, counts, n_tiles, tile):
    blk = MOE_BLOCK
    a = ids.shape[0] * MOE_TOPK
    padded = (counts + blk - 1) // blk * blk
    pad_ends = jnp.cumsum(padded)
    pad_starts = pad_ends - padded
    onehot = ids[:, :, None] == jnp.arange(MOE_EXPERTS, dtype=jnp.int32)
    dest = jnp.sum(jnp.where(onehot, pad_starts, 0), axis=-1) + rank
    dest_tiles = dest.reshape(n_tiles, tile, MOE_TOPK).transpose(0, 2, 1).reshape(n_tiles, 1, MOE_TOPK * tile)
    nb = -(-(a + MOE_EXPERTS * (blk - 1)) // blk)
    block_start = jnp.arange(nb, dtype=jnp.int32) * blk
    block_exp = jnp.minimum(jnp.sum(pad_ends[None, :] <= block_start[:, None], axis=1), MOE_EXPERTS - 1)
    n_used = (pad_ends[-1] // blk).reshape(1)
    fill = jnp.stack([pad_starts + counts, padded - counts,
                      jnp.broadcast_to(n_used, (MOE_EXPERTS,))])
    first = jnp.concatenate([jnp.ones((1,), bool), block_exp[1:] != block_exp[:-1]])
    slot = (jnp.cumsum(first) - 1) % 2
    e_ids = jnp.arange(MOE_EXPERTS, dtype=jnp.int32)
    later = jnp.where((counts > 0)[None, :] & (e_ids[None, :] > e_ids[:, None]), e_ids[None, :], MOE_EXPERTS)
    nxt = jnp.min(later, axis=1)
    nxt = jnp.where(nxt < MOE_EXPERTS, nxt, -1)[block_exp]
    sched = jnp.stack([block_exp, first.astype(jnp.int32), slot, nxt])
    return dest_tiles.astype(jnp.int32), sched.astype(jnp.int32), n_used.astype(jnp.int32), fill.astype(jnp.int32)


def _dispatch_kernel(fill_ref, dest_ref, h_ref, xs_hbm, zbuf, sem, zsem):
    tm = h_ref.shape[0]
    i = pl.program_id(0)

    def fill_copies(start):
        def issue(cp, on):
            @pl.when(on)
            def _():
                cp.start() if start else cp.wait()

        for e in range(MOE_EXPERTS):
            off = fill_ref[0, e]
            n_pad = fill_ref[1, e]
            bit = MOE_BLOCK // 2
            while bit:
                issue(pltpu.make_async_copy(zbuf.at[pl.ds(0, bit)], xs_hbm.at[pl.ds(off, bit)], zsem),
                      (n_pad & bit) != 0)
                off = off + (n_pad & bit)
                bit //= 2

        half = zbuf.shape[0]
        first = fill_ref[2, 0] * (MOE_BLOCK // half)

        def tail(j, c):
            cp = pltpu.make_async_copy(zbuf, xs_hbm.at[pl.ds(j * half, half)], zsem)
            cp.start() if start else cp.wait()
            return c
        lax.fori_loop(first, xs_hbm.shape[0] // half, tail, 0)

    @pl.when(i == 0)
    def _():
        zbuf[...] = jnp.zeros_like(zbuf)
        fill_copies(True)
        fill_copies(False)

    def row_copy(j, dst):
        return pltpu.make_async_copy(h_ref.at[pl.ds(j % tm, 1)], xs_hbm.at[pl.ds(dst, 1)], sem)

    for j in range(MOE_TOPK * tm):
        row_copy(j, dest_ref[0, 0, j]).start(priority=j % 2)

    def wait(j, c):
        row_copy(0, 0).wait()
        return c
    lax.fori_loop(0, MOE_TOPK * tm, wait, 0, unroll=8)


def _dispatch(h2, dest_tiles, fill, n_rows):
    t, sub, lanes = h2.shape
    nt, _, two_tm = dest_tiles.shape
    tm = two_tm // MOE_TOPK
    grid_spec = pltpu.PrefetchScalarGridSpec(
        num_scalar_prefetch=1,
        grid=(nt,),
        in_specs=[pl.BlockSpec((1, 1, two_tm), lambda i, f: (i, 0, 0), memory_space=pltpu.SMEM),
                  pl.BlockSpec((tm, sub, lanes), lambda i, f: (i, 0, 0))],
        out_specs=pl.BlockSpec(memory_space=pl.ANY),
        scratch_shapes=[pltpu.VMEM((MOE_BLOCK // 2, sub, lanes), BF16),
                        pltpu.SemaphoreType.DMA(()),
                        pltpu.SemaphoreType.DMA(())],
    )
    return pl.pallas_call(
        _dispatch_kernel,
        grid_spec=grid_spec,
        out_shape=jax.ShapeDtypeStruct((n_rows, sub, lanes), BF16),
        compiler_params=_params("arbitrary"),
        name="moe_dispatch",
    )(fill, dest_tiles, h2)


def _experts_kernel(sched_ref, nused_ref, x_ref, w1_hbm, w3_hbm, w2_hbm, y_ref,
                    w1f, w3f, w2f, w1b, w3b, w2b, wsem, *, layer):
    i = pl.program_id(0)
    used = i < nused_ref[0]
    e, first, slot, e_next = (sched_ref[r, i] for r in range(4))

    def weight_copies(ex, s):
        return [pltpu.make_async_copy(src.at[layer, ex], dst.at[s], wsem.at[s])
                for src, dst in ((w1_hbm, w1f), (w3_hbm, w3f), (w2_hbm, w2f))]

    @pl.when(i == 0)
    def _():
        for cp in weight_copies(e, 0):
            cp.start()

    @pl.when(used & (first == 1))
    def _():
        for cp in weight_copies(e, slot):
            cp.wait()

        @pl.when(e_next >= 0)
        def _():
            for cp in weight_copies(e_next, 1 - slot):
                cp.start()

        w1b[...] = w1f[slot].astype(BF16)
        w3b[...] = w3f[slot].astype(BF16)
        w2b[...] = w2f[slot].astype(BF16)

    @pl.when(used)
    def _():
        blk, sub, lanes = x_ref.shape
        x = x_ref[...].reshape(blk, sub * lanes)
        a = _dot(x, w1b[...])
        hb = (a * _sigmoid(a)) * _dot(x, w3b[...])
        y_ref[...] = _dot(hb.astype(BF16), w2b[...]).astype(BF16).reshape(blk, sub, lanes)

    @pl.when(jnp.logical_not(used))
    def _():
        y_ref[...] = jnp.zeros_like(y_ref)


def _experts(xs, sched, n_used, w1, w3, w2, layer):
    nb = sched.shape[1]
    _, sub, lanes = xs.shape
    d = sub * lanes
    hid = w1.shape[-1]
    blk = MOE_BLOCK

    def xmap(i, sc, nu):
        return (jnp.minimum(i, jnp.maximum(nu[0], 1) - 1), 0, 0)

    grid_spec = pltpu.PrefetchScalarGridSpec(
        num_scalar_prefetch=2,
        grid=(nb,),
        in_specs=[pl.BlockSpec((blk, sub, lanes), xmap),
                  pl.BlockSpec(memory_space=pl.ANY),
                  pl.BlockSpec(memory_space=pl.ANY),
                  pl.BlockSpec(memory_space=pl.ANY)],
        out_specs=pl.BlockSpec((blk, sub, lanes), lambda i, sc, nu: (i, 0, 0)),
        scratch_shapes=[pltpu.VMEM((2, d, hid), F32),
                        pltpu.VMEM((2, d, hid), F32),
                        pltpu.VMEM((2, hid, d), F32),
                        pltpu.VMEM((d, hid), BF16),
                        pltpu.VMEM((d, hid), BF16),
                        pltpu.VMEM((hid, d), BF16),
                        pltpu.SemaphoreType.DMA((2,))],
    )
    return pl.pallas_call(
        functools.partial(_experts_kernel, layer=layer),
        grid_spec=grid_spec,
        out_shape=jax.ShapeDtypeStruct((nb * blk, sub, lanes), BF16),
        compiler_params=_params("arbitrary"),
        name="moe_experts",
    )(sched, n_used, xs, w1, w3, w2)


def _combine_kernel(d0_ref, d1_ref, d2_ref, x_ref, gt_ref, gate_ref, post_ref, sh_ref, sc_ref, ys_hbm,
                    o_ref, *rest, final):
    hn_ref = None if final else rest[0]
    ybuf, sem = rest[-2:]
    tm = x_ref.shape[0] // 2
    d = x_ref.shape[1]
    n_rows = MOE_TOPK * tm
    i = pl.program_id(0)

    def row_copy(row, j, s):
        return pltpu.make_async_copy(ys_hbm.at[pl.ds(row, 1)], ybuf.at[s, pl.ds(j, 1)], sem.at[s])

    def start_rows(dst_ref, s, lo, hi):
        for j in range(lo, hi):
            row_copy(dst_ref[0, 0, j], j, s).start(priority=j % 2)

    def wait_all(s):
        def body(j, c):
            row_copy(0, 0, s).wait()
            return c
        lax.fori_loop(0, n_rows, body, 0, unroll=8)

    def tile(s, next_dst_ref):
        n_pieces = tm // COMBINE_PIECE_ROWS
        per = n_rows // n_pieces
        for p in range(n_pieces):
            start_rows(next_dst_ref, 1 - s, p * per, (p + 1) * per)
            lo = p * COMBINE_PIECE_ROWS
            rs = pl.ds(s * tm + lo, COMBINE_PIECE_ROWS)
            gt = gt_ref[rs, :]
            y0 = ybuf[s, pl.ds(lo, COMBINE_PIECE_ROWS)].reshape(COMBINE_PIECE_ROWS, d).astype(F32)
            y1 = ybuf[s, pl.ds(tm + lo, COMBINE_PIECE_ROWS)].reshape(COMBINE_PIECE_ROWS, d).astype(F32)
            x = x_ref[rs, :] + gate_ref[0] * (gt[:, 0:1] * y0 + gt[:, 1:2] * y1)
            if final:
                x = x * lax.rsqrt(jnp.mean(x * x, axis=-1, keepdims=True) + EPS) * post_ref[...]
            else:
                hn_ref[rs, :] = _norm_mod(x, post_ref[...], sh_ref[0], sc_ref[0]).astype(BF16)
            o_ref[rs, :] = x

    @pl.when(i == 0)
    def _():
        start_rows(d0_ref, 0, 0, n_rows)

    wait_all(0)
    tile(0, d1_ref)
    wait_all(1)
    tile(1, d2_ref)

    @pl.when(i == pl.num_programs(0) - 1)
    def _():
        wait_all(0)


def _combine(x, ys, dest_tiles, gates, gate, post_gain, post_shift, post_scale, rows_per_batch, final):
    t, d = x.shape
    bsz = gate.shape[0]
    nt, _, two_tm = dest_tiles.shape
    tm = two_tm // MOE_TOPK
    tpb = rows_per_batch // tm
    assert nt % 2 == 0 and tpb % 2 == 0
    smem = functools.partial(pl.BlockSpec, memory_space=pltpu.SMEM)
    per_batch = lambda i: (2 * i // tpb, 0, 0)
    rows = pl.BlockSpec((2 * tm, d), lambda i: (i, 0))
    out_specs, out_shape = [rows], [jax.ShapeDtypeStruct((t, d), F32)]
    if not final:
        out_specs.append(rows)
        out_shape.append(jax.ShapeDtypeStruct((t, d), BF16))
    return pl.pallas_call(
        functools.partial(_combine_kernel, final=final),
        grid=(nt // 2,),
        in_specs=[smem((1, 1, two_tm), lambda i: (2 * i, 0, 0)),
                  smem((1, 1, two_tm), lambda i: (2 * i + 1, 0, 0)),
                  smem((1, 1, two_tm), lambda i: ((2 * i + 2) % nt, 0, 0)),
                  rows,
                  pl.BlockSpec((2 * tm, LANES), lambda i: (i, 0)),
                  pl.BlockSpec((1, 1, d), per_batch),
                  pl.BlockSpec((1, d), lambda i: (0, 0)),
                  pl.BlockSpec((1, 1, d), per_batch),
                  pl.BlockSpec((1, 1, d), per_batch),
                  pl.BlockSpec(memory_space=pl.ANY)],
        out_specs=out_specs,
        out_shape=out_shape,
        scratch_shapes=[pltpu.VMEM((2, two_tm) + ys.shape[1:], BF16),
                        pltpu.SemaphoreType.DMA((2,))],
        compiler_params=_params("arbitrary"),
        name="moe_combine",
    )(dest_tiles, dest_tiles, dest_tiles, x, gates, gate.reshape(bsz, 1, d), post_gain.reshape(1, d),
      post_shift.reshape(bsz, 1, d), post_scale.reshape(bsz, 1, d), ys)


def _moe(x, routed, gate, w1, w3, w2, layer, post, seq, final):
    t, _ = x.shape
    tile = _pick(seq, MOE_TOKEN_TILE)
    h2, meta, gts, cnt = routed
    dest_tiles, sched, n_used, fill = _dispatch_tables(
        meta[:, :MOE_TOPK], meta[:, MOE_TOPK:2 * MOE_TOPK], cnt[0, :MOE_EXPERTS], t // tile, tile)
    xs = _dispatch(h2, dest_tiles, fill, sched.shape[1] * MOE_BLOCK)
    ys = _experts(xs, sched, n_used, w1, w3, w2, layer)
    return _combine(x, ys, dest_tiles, gts, gate, *post, seq, final)


def kernel(x, c, norm_mix, norm_ffn, ada_w, ada_b, gla_w_in, gla_b_r, gla_w_a_up, gla_b_a, gla_norm_g, gla_w_out, sgu_w_in, sgu_b_in, sgu_ln_g, sgu_ln_b, sgu_w_s, sgu_b_s, sgu_w_out, moe_w_grp, moe_b_grp, moe_w_exp, moe_b_exp, moe_w1, moe_w3, moe_w2, final_norm):
    bsz, seq, d = x.shape
    depth = norm_mix.shape[0]
    xt = x.reshape(bsz * seq, d)
    mod = _adaln(c, ada_w, ada_b)
    mods = [[mod[i, :, m * d:(m + 1) * d] for m in range(N_MOD)] for i in range(depth)]
    mixer_in = xt
    for i in range(depth):
        sh1, sc1, g1, sh2, sc2, g2 = mods[i]
        j = i // 2
        if i % 2 == 0:
            w_in = gla_w_in[j]
            n_main = w_in.shape[1] - GLA_RANK
            w_main = w_in[:, :n_main].astype(BF16)
            w_low = jnp.zeros((d, LANES), F32).at[:, :GLA_RANK].set(w_in[:, n_main:]).astype(BF16)
            proj, a_low = _norm_matmul(mixer_in, norm_mix[i], sh1, sc1, w_main, seq, "gla_in_proj", w_side=w_low)
            o = _gla_scan(proj, a_low, gla_w_a_up[j], gla_b_a[j], gla_b_r[j], gla_norm_g[j], bsz, seq)
            w_out, name = gla_w_out[j], "gla_out_proj"
        else:
            z, = _norm_matmul(mixer_in, norm_mix[i], sh1, sc1, sgu_w_in[j].astype(BF16), seq, "sgu_in_proj",
                              bias=sgu_b_in[j])
            o = _sgu_spatial(z, sgu_ln_g[j], sgu_ln_b[j], sgu_w_s[j], sgu_b_s[j])
            w_out, name = sgu_w_out[j], "sgu_out_proj"
        xt = _matmul_residual(o, w_out.astype(BF16), xt, g1, seq, name)
        routed = _router(xt, norm_ffn[i], sh2, sc2, moe_w_grp[i], moe_b_grp[i], moe_w_exp[i], moe_b_exp[i], seq)
        final = i == depth - 1
        post = (final_norm, sh2, sc2) if final else (norm_mix[i + 1], mods[i + 1][0], mods[i + 1][1])
        xt, *nxt = _moe(xt, routed, g2, moe_w1, moe_w3, moe_w2, i, post, seq, final)
        mixer_in = xt if final else nxt[0]
    return xt.reshape(bsz, seq, d)
```

```python
import functools
import math

import jax
import jax.numpy as jnp
import numpy as np
from jax import lax
from jax.experimental import pallas as pl
from jax.experimental.pallas import tpu as pltpu

F32 = jnp.float32
BF16 = jnp.bfloat16

EPS = 1e-6
N_MOD = 6
GLA_HEADS = 4
GLA_RANK = 16
GLA_TAU = 16.0
GLA_CHUNK = 256
GLA_FLAT = 8
GLA_HEADS_PER_STEP = 4
SGU_CHUNK = 128
SGU_GROUPS = 8
MOE_GROUPS = 8
MOE_PER_GROUP = 8
MOE_EXPERTS = MOE_GROUPS * MOE_PER_GROUP
MOE_TOPK = 2
MOE_BLOCK = 256
MOE_TOKEN_TILE = 256
COMBINE_PIECE_ROWS = 16
LANES = 128
V7X_VMEM_LIMIT_BYTES = 56 * 2**20


def _params(*semantics):
    return pltpu.CompilerParams(dimension_semantics=semantics,
                                vmem_limit_bytes=V7X_VMEM_LIMIT_BYTES)


def _dot(a, b):
    return jnp.dot(a, b, preferred_element_type=F32)


def _dot_nt(a, b):
    return lax.dot_general(a, b, (((1,), (1,)), ((), ())), preferred_element_type=F32)


def _dot_tn(a, b):
    return lax.dot_general(a, b, (((0,), (0,)), ((), ())), preferred_element_type=F32)


def _split3(x):
    hi = x.astype(BF16)
    r1 = x - hi.astype(F32)
    mid = r1.astype(BF16)
    lo = (r1 - mid.astype(F32)).astype(BF16)
    return hi, mid, lo


def _sigmoid(x):
    return 1.0 / (1.0 + jnp.exp(-x))


def _gelu_tanh(x):
    c = math.sqrt(2.0 / math.pi)
    return x * (0.5 * (1.0 + jnp.tanh(c * (x + 0.044715 * (x * x * x)))))


def _pick(n, pref):
    t = min(n, pref)
    while n % t:
        t //= 2
    return t


def _adaln_kernel(c_ref, w_ref, b_ref, o_ref):
    c = c_ref[...]
    ca = c * _sigmoid(c)
    hi, mid, _ = _split3(ca)
    w = w_ref[0].astype(BF16)
    o_ref[0] = _dot(hi, w) + _dot(mid, w) + b_ref[0]


def _adaln(c, ada_w, ada_b):
    depth, d, n = ada_w.shape
    b = c.shape[0]
    rows = 8
    cp = jnp.zeros((rows, d), F32).at[:b].set(c)
    tn = _pick(n, 1024)
    out = pl.pallas_call(
        _adaln_kernel,
        grid=(depth, n // tn),
        in_specs=[pl.BlockSpec((rows, d), lambda l, j: (0, 0)),
                  pl.BlockSpec((1, d, tn), lambda l, j: (l, 0, j)),
                  pl.BlockSpec((1, 1, tn), lambda l, j: (l, 0, j))],
        out_specs=pl.BlockSpec((1, rows, tn), lambda l, j: (l, 0, j)),
        out_shape=jax.ShapeDtypeStruct((depth, rows, n), F32),
        compiler_params=_params("parallel", "parallel"),
        name="adaln",
    )(cp, ada_w, ada_b.reshape(depth, 1, n))
    return out[:, :b]


def _norm_mod(x, gain, shift, scale):
    ms = jnp.mean(x * x, axis=-1, keepdims=True)
    return (x * lax.rsqrt(ms + EPS) * gain) * (1.0 + scale) + shift


def _norm_matmul_kernel(x_ref, g_ref, sh_ref, sc_ref, w_ref, *rest, gelu, side, normed):
    rest = list(rest)
    bias_ref = rest.pop(0) if gelu else None
    wside_ref = rest.pop(0) if side else None
    o_ref = rest.pop(0)
    oside_ref = rest.pop(0) if side else None
    h_ref = x_ref if normed else rest.pop(0)

    if not normed or side:
        @pl.when(pl.program_id(1) == 0)
        def _():
            if not normed:
                h_ref[...] = _norm_mod(x_ref[...], g_ref[...], sh_ref[0], sc_ref[0]).astype(BF16)
            if side:
                oside_ref[...] = _dot(h_ref[...], wside_ref[...])

    acc = _dot(h_ref[...], w_ref[...])
    if gelu:
        acc = _gelu_tanh(acc + bias_ref[...])
    o_ref[...] = acc.astype(o_ref.dtype)


def _norm_matmul(x, gain, shift, scale, w, rows_per_batch, name, bias=None, w_side=None, tm=1024, tn=1024):
    normed = x.dtype == BF16
    t, d = x.shape
    n = w.shape[1]
    tm = _pick(rows_per_batch, tm)
    tn = _pick(n, tn)
    tpb = rows_per_batch // tm
    bsz = shift.shape[0]
    in_specs = [pl.BlockSpec((tm, d), lambda i, j: (i, 0)),
                pl.BlockSpec((1, d), lambda i, j: (0, 0)),
                pl.BlockSpec((1, 1, d), lambda i, j: (i // tpb, 0, 0)),
                pl.BlockSpec((1, 1, d), lambda i, j: (i // tpb, 0, 0)),
                pl.BlockSpec((d, tn), lambda i, j: (0, j))]
    args = [x, gain.reshape(1, d), shift.reshape(bsz, 1, d), scale.reshape(bsz, 1, d), w]
    out_specs = [pl.BlockSpec((tm, tn), lambda i, j: (i, j))]
    out_shape = [jax.ShapeDtypeStruct((t, n), BF16)]
    if bias is not None:
        in_specs.append(pl.BlockSpec((1, tn), lambda i, j: (0, j)))
        args.append(bias.reshape(1, n))
    if w_side is not None:
        in_specs.append(pl.BlockSpec((d, LANES), lambda i, j: (0, 0)))
        args.append(w_side)
        out_specs.append(pl.BlockSpec((tm, LANES), lambda i, j: (i, 0)))
        out_shape.append(jax.ShapeDtypeStruct((t, LANES), F32))
    return pl.pallas_call(
        functools.partial(_norm_matmul_kernel, gelu=bias is not None, side=w_side is not None, normed=normed),
        grid=(t // tm, n // tn),
        in_specs=in_specs,
        out_specs=out_specs,
        out_shape=out_shape,
        scratch_shapes=[] if normed else [pltpu.VMEM((tm, d), BF16)],
        compiler_params=_params("parallel", "arbitrary"),
        name=name,
    )(*args)


def _matmul_residual_kernel(a_ref, w_ref, x_ref, gate_ref, o_ref):
    o_ref[...] = x_ref[...] + gate_ref[0] * _dot(a_ref[...], w_ref[...])


def _matmul_residual(a, w, x, gate, rows_per_batch, name, tm=512):
    t, k = a.shape
    n = w.shape[1]
    tm = _pick(rows_per_batch, tm)
    tpb = rows_per_batch // tm
    bsz = gate.shape[0]
    return pl.pallas_call(
        _matmul_residual_kernel,
        grid=(t // tm,),
        in_specs=[pl.BlockSpec((tm, k), lambda i: (i, 0)),
                  pl.BlockSpec((k, n), lambda i: (0, 0), pipeline_mode=pl.Buffered(1)),
                  pl.BlockSpec((tm, n), lambda i: (i, 0)),
                  pl.BlockSpec((1, 1, n), lambda i: (i // tpb, 0, 0))],
        out_specs=pl.BlockSpec((tm, n), lambda i: (i, 0)),
        out_shape=jax.ShapeDtypeStruct((t, n), F32),
        compiler_params=_params("parallel"),
        name=name,
    )(a, w, x, gate.reshape(bsz, 1, n))


def _rows_bcast(b, n, off):
    c, dk = b.shape
    parts = [jnp.broadcast_to(b[j * n + off:j * n + off + 1, :], (n, dk)) for j in range(c // n)]
    return parts[0] if len(parts) == 1 else jnp.concatenate(parts, axis=0)


def _pair_levels(c):
    t = np.arange(c)[:, None]
    s = np.arange(c)[None, :]
    x = t ^ s
    lvl = np.floor(np.log2(np.maximum(x, 1))).astype(np.int32) + 1
    lvl = np.where(x < GLA_FLAT, 0, lvl)
    return np.where(s <= t, lvl, -1).astype(np.int32)


def _gla_scan_kernel(q_ref, k_ref, v_ref, r_ref, al_ref, wup_ref, ba_ref, br_ref, ng_ref, lvl_ref,
                     o_ref, st_ref, *, scale):
    c = q_ref.shape[0]
    heads, dv, dk = st_ref.shape

    @pl.when(pl.program_id(2) == 0)
    def _():
        st_ref[...] = jnp.zeros_like(st_ref)

    al = _split3(al_ref[:, :GLA_RANK])
    wu = _split3(wup_ref[...])
    pre = _dot(al[0], wu[0]) + _dot(al[0], wu[1]) + _dot(al[1], wu[0]) + ba_ref[...]
    g = (jnp.minimum(pre, 0.0) - jnp.log(1.0 + jnp.exp(-jnp.abs(pre)))) * (math.log2(math.e) / GLA_TAU)

    lvl = lvl_ref[...]
    tri = jnp.where(lvl >= 0, 1.0, 0.0).astype(BF16)
    gs = _split3(g)
    b_all = _dot(tri, gs[0]) + _dot(tri, gs[1]) + _dot(tri, gs[2])

    for hd in range(heads):
        ks = pl.ds(hd * dk, dk)
        vs = pl.ds(hd * dv, dv)
        b = b_all[:, hd * dk:(hd + 1) * dk]
        q = q_ref[:, ks] * jnp.asarray(scale, BF16)
        k = k_ref[:, ks]
        v = v_ref[:, vs]

        a = jnp.zeros((c, c), F32)
        n = c
        while n > GLA_FLAT:
            e = jnp.exp2(-jnp.abs(b - _rows_bcast(b, n, n // 2))).astype(BF16)
            a = jnp.where(lvl == n.bit_length() - 1, _dot_nt(q * e, k * e), a)
            n //= 2
        ref = _rows_bcast(b, GLA_FLAT, 0)
        qh = q * jnp.exp2(b - ref).astype(BF16)
        kh = k * jnp.exp2(ref - b).astype(BF16)
        a = jnp.where(lvl == 0, _dot_nt(qh, kh), a)

        st = st_ref[hd]
        o = _dot(a.astype(BF16), v) + _dot_nt(q * jnp.exp2(b).astype(BF16), st.astype(BF16))

        b_last = b[c - 1:c, :]
        kd = k * jnp.exp2(b_last - b).astype(BF16)
        st_ref[hd] = st * jnp.exp2(b_last) + _dot_tn(v, kd)

        on = o * lax.rsqrt(jnp.mean(o * o, axis=-1, keepdims=True) + EPS) * ng_ref[...]
        r = r_ref[:, vs].astype(F32) + br_ref[:, vs]
        o_ref[:, vs] = (r * _sigmoid(r) * on).astype(o_ref.dtype)


def _gla_scan(proj, a_low, w_a_up, b_a, b_r, norm_g, bsz, seq):
    t, n3 = proj.shape
    dkk = n3 // 6
    dvv = 2 * dkk
    h = GLA_HEADS
    dk, dv = dkk // h, dvv // h
    c = _pick(seq, GLA_CHUNK)
    nc = seq // c
    hp = GLA_HEADS_PER_STEP
    gk, gv = hp * dk, hp * dv
    voff = (2 * dkk) // gv
    roff = (2 * dkk + dvv) // gv
    rowmap = lambda b, hh, n: b * nc + n
    return pl.pallas_call(
        functools.partial(_gla_scan_kernel, scale=float(dk) ** -0.5),
        grid=(bsz, h // hp, nc),
        in_specs=[pl.BlockSpec((c, gk), lambda b, hh, n: (rowmap(b, hh, n), hh)),
                  pl.BlockSpec((c, gk), lambda b, hh, n: (rowmap(b, hh, n), h // hp + hh)),
                  pl.BlockSpec((c, gv), lambda b, hh, n: (rowmap(b, hh, n), voff + hh)),
                  pl.BlockSpec((c, gv), lambda b, hh, n: (rowmap(b, hh, n), roff + hh)),
                  pl.BlockSpec((c, LANES), lambda b, hh, n: (rowmap(b, hh, n), 0)),
                  pl.BlockSpec((GLA_RANK, gk), lambda b, hh, n: (0, hh)),
                  pl.BlockSpec((1, gk), lambda b, hh, n: (0, hh)),
                  pl.BlockSpec((1, gv), lambda b, hh, n: (0, hh)),
                  pl.BlockSpec((1, dv), lambda b, hh, n: (0, 0)),
                  pl.BlockSpec((c, c), lambda b, hh, n: (0, 0))],
        out_specs=pl.BlockSpec((c, gv), lambda b, hh, n: (rowmap(b, hh, n), hh)),
        out_shape=jax.ShapeDtypeStruct((t, dvv), BF16),
        scratch_shapes=[pltpu.VMEM((hp, dv, dk), F32)],
        compiler_params=_params("parallel", "parallel", "arbitrary"),
        name="gla_scan",
    )(proj, proj, proj, proj, a_low, w_a_up, b_a.reshape(1, dkk), b_r.reshape(1, dvv),
      norm_g.reshape(1, dv), jnp.asarray(_pair_levels(c)))


def _sgu_spatial_kernel(u_ref, v_ref, lg_ref, lb_ref, ws_ref, bs_ref, o_ref, wc_scr):
    rows, half = v_ref.shape
    c = SGU_CHUNK
    gw = half // SGU_GROUPS

    @pl.when(pl.program_id(0) == 0)
    def _():
        row = lax.broadcasted_iota(jnp.int32, (c, c), 0)
        col = lax.broadcasted_iota(jnp.int32, (c, c), 1)
        for g in range(SGU_GROUPS):
            wc_scr[g] = jnp.where(row >= col, ws_ref[g], 0.0).astype(BF16)

    for ci in range(rows // c):
        rs = pl.ds(ci * c, c)
        v = v_ref[rs, :].astype(F32)
        mu = jnp.mean(v, axis=-1, keepdims=True)
        vc = v - mu
        var = jnp.mean(vc * vc, axis=-1, keepdims=True)
        vn = (vc * lax.rsqrt(var + EPS) * lg_ref[...] + lb_ref[...]).astype(BF16)
        for g in range(SGU_GROUPS):
            cs = pl.ds(g * gw, gw)
            mixed = _dot(wc_scr[g], vn[:, g * gw:(g + 1) * gw]) + bs_ref[:, g:g + 1]
            o_ref[rs, cs] = (u_ref[rs, cs].astype(F32) * mixed).astype(o_ref.dtype)


def _sgu_spatial(z, ln_g, ln_b, w_s, b_s, rows=512):
    t, two_half = z.shape
    half = two_half // 2
    c = SGU_CHUNK
    rows = max(c, _pick(t, rows))
    return pl.pallas_call(
        _sgu_spatial_kernel,
        grid=(t // rows,),
        in_specs=[pl.BlockSpec((rows, half), lambda i: (i, 0)),
                  pl.BlockSpec((rows, half), lambda i: (i, 1)),
                  pl.BlockSpec((1, half), lambda i: (0, 0)),
                  pl.BlockSpec((1, half), lambda i: (0, 0)),
                  pl.BlockSpec((SGU_GROUPS, c, c), lambda i: (0, 0, 0)),
                  pl.BlockSpec((c, SGU_GROUPS), lambda i: (0, 0))],
        out_specs=pl.BlockSpec((rows, half), lambda i: (i, 0)),
        out_shape=jax.ShapeDtypeStruct((t, half), BF16),
        scratch_shapes=[pltpu.VMEM((SGU_GROUPS, c, c), BF16)],
        compiler_params=_params("arbitrary"),
        name="sgu_spatial",
    )(z, z, ln_g.reshape(1, half), ln_b.reshape(1, half), w_s, b_s.T)


def _route(h, w, bias, cnt):
    hi, mid, _ = _split3(h)
    l2 = _dot(hi, w) + _dot(mid, w)
    logits = l2[:, :LANES] + l2[:, LANES:] + bias
    lane = lax.broadcasted_iota(jnp.int32, logits.shape, 1)
    lane_f = lane.astype(F32)
    neg = -jnp.inf
    big = float(LANES)

    grp = jnp.where(lane < MOE_GROUPS, logits, neg)
    gmax = jnp.max(grp, axis=-1, keepdims=True)
    gidx = jnp.min(jnp.where(grp == gmax, lane_f, big), axis=-1, keepdims=True)
    g_w = 1.0 / jnp.sum(jnp.exp(grp - gmax), axis=-1, keepdims=True)

    e_lane = lane - MOE_GROUPS
    in_grp = (e_lane >= 0) & (e_lane < MOE_EXPERTS) & ((e_lane // MOE_PER_GROUP).astype(F32) == gidx)
    el = jnp.where(in_grp, logits, neg)
    m1 = jnp.max(el, axis=-1, keepdims=True)
    i1 = jnp.min(jnp.where(el == m1, lane_f, big), axis=-1, keepdims=True)
    el2 = jnp.where(lane_f == i1, neg, el)
    m2 = jnp.max(el2, axis=-1, keepdims=True)
    i2 = jnp.min(jnp.where(el2 == m2, lane_f, big), axis=-1, keepdims=True)
    z = jnp.sum(jnp.exp(el - m1), axis=-1, keepdims=True)
    p1 = 1.0 / z
    p2 = jnp.exp(m2 - m1) / z
    psum = p1 + p2
    e1 = i1 - float(MOE_GROUPS)
    e2 = i2 - float(MOE_GROUPS)

    tm = logits.shape[0]
    oh1 = (lane_f == e1).astype(F32)
    oh2 = (lane_f == e2).astype(F32)
    oh = oh1 + oh2
    row = lax.broadcasted_iota(jnp.int32, (tm, tm), 0)
    col = lax.broadcasted_iota(jnp.int32, (tm, tm), 1)
    before = _dot((row > col).astype(BF16), oh.astype(BF16)) + cnt
    r1 = jnp.sum(oh1 * before, axis=-1, keepdims=True)
    r2 = jnp.sum(oh2 * before, axis=-1, keepdims=True)
    cnt = cnt + jnp.sum(oh, axis=0, keepdims=True)

    meta = jnp.where(lane == 0, e1, jnp.where(lane == 1, e2, jnp.where(lane == 2, r1, jnp.where(lane == 3, r2, 0.0))))
    gates = jnp.where(lane == 0, g_w * (p1 / psum), jnp.where(lane == 1, g_w * (p2 / psum), 0.0))
    return hi, meta.astype(jnp.int32), gates, cnt


def _router_kernel(x_ref, g_ref, sh_ref, sc_ref, w_ref, b_ref, h_ref, id_ref, gt_ref, cnt_ref, cnt_scr):
    @pl.when(pl.program_id(0) == 0)
    def _():
        cnt_scr[...] = jnp.zeros_like(cnt_scr)

    h = _norm_mod(x_ref[...], g_ref[...], sh_ref[0], sc_ref[0])
    hi, meta, gates, cnt = _route(h, w_ref[...], b_ref[...], cnt_scr[...])
    h_ref[...] = hi.reshape(h_ref.shape)
    id_ref[...] = meta
    gt_ref[...] = gates
    cnt_scr[...] = cnt
    cnt_ref[...] = jnp.broadcast_to(cnt, cnt_ref.shape).astype(jnp.int32)


def _router(x, gain, shift, scale, w_grp, b_grp, w_exp, b_exp, rows_per_batch, tm=512):
    t, d = x.shape
    bsz = shift.shape[0]
    tm = _pick(rows_per_batch, tm)
    tpb = rows_per_batch // tm
    n_log = MOE_GROUPS + MOE_EXPERTS
    wcat = jnp.zeros((d, LANES), F32).at[:, :MOE_GROUPS].set(w_grp).at[:, MOE_GROUPS:n_log].set(w_exp)
    w_hi = wcat.astype(BF16)
    w_lo = (wcat - w_hi.astype(F32)).astype(BF16)
    w2 = jnp.concatenate([w_hi, w_lo], axis=1)
    bcat = jnp.zeros((1, LANES), F32).at[0, :MOE_GROUPS].set(b_grp).at[0, MOE_GROUPS:n_log].set(b_exp)
    return pl.pallas_call(
        _router_kernel,
        grid=(t // tm,),
        in_specs=[pl.BlockSpec((tm, d), lambda i: (i, 0)),
                  pl.BlockSpec((1, d), lambda i: (0, 0)),
                  pl.BlockSpec((1, 1, d), lambda i: (i // tpb, 0, 0)),
                  pl.BlockSpec((1, 1, d), lambda i: (i // tpb, 0, 0)),
                  pl.BlockSpec((d, 2 * LANES), lambda i: (0, 0)),
                  pl.BlockSpec((1, LANES), lambda i: (0, 0))],
        out_specs=[pl.BlockSpec((tm, d // LANES, LANES), lambda i: (i, 0, 0)),
                   pl.BlockSpec((tm, LANES), lambda i: (i, 0)),
                   pl.BlockSpec((tm, LANES), lambda i: (i, 0)),
                   pl.BlockSpec((8, LANES), lambda i: (0, 0))],
        out_shape=[jax.ShapeDtypeStruct((t, d // LANES, LANES), BF16),
                   jax.ShapeDtypeStruct((t, LANES), jnp.int32),
                   jax.ShapeDtypeStruct((t, LANES), F32),
                   jax.ShapeDtypeStruct((8, LANES), jnp.int32)],
        scratch_shapes=[pltpu.VMEM((1, LANES), F32)],
        compiler_params=_params("arbitrary"),
        name="moe_router",
    )(x, gain.reshape(1, d), shift.reshape(bsz, 1, d), scale.reshape(bsz, 1, d), w2, bcat)


def _dispatch_tables(ids, rank, counts, n_tiles, tile):
    blk = MOE_BLOCK
    a = ids.shape[0] * MOE_TOPK
    padded = (counts + blk - 1) // blk * blk
    pad_ends = jnp.cumsum(padded)
    pad_starts = pad_ends - padded
    onehot = ids[:, :, None] == jnp.arange(MOE_EXPERTS, dtype=jnp.int32)
    dest = jnp.sum(jnp.where(onehot, pad_starts, 0), axis=-1) + rank
    dest_tiles = dest.reshape(n_tiles, tile, MOE_TOPK).transpose(0, 2, 1).reshape(n_tiles, 1, MOE_TOPK * tile)
    nb = -(-(a + MOE_EXPERTS * (blk - 1)) // blk)
    block_start = jnp.arange(nb, dtype=jnp.int32) * blk
    block_exp = jnp.minimum(jnp.sum(pad_ends[None, :] <= block_start[:, None], axis=1), MOE_EXPERTS - 1)
    n_used = (pad_ends[-1] // blk).reshape(1)
    fill = jnp.stack([pad_starts + counts, padded - counts,
                      jnp.broadcast_to(n_used, (MOE_EXPERTS,))])
    first = jnp.concatenate([jnp.ones((1,), bool), block_exp[1:] != block_exp[:-1]])
    slot = (jnp.cumsum(first) - 1) % 2
    e_ids = jnp.arange(MOE_EXPERTS, dtype=jnp.int32)
    later = jnp.where((counts > 0)[None, :] & (e_ids[None, :] > e_ids[:, None]), e_ids[None, :], MOE_EXPERTS)
    nxt = jnp.min(later, axis=1)
    nxt = jnp.where(nxt < MOE_EXPERTS, nxt, -1)[block_exp]
    sched = jnp.stack([block_exp, first.astype(jnp.int32), slot, nxt])
    return dest_tiles.astype(jnp.int32), sched.astype(jnp.int32), n_used.astype(jnp.int32), fill.astype(jnp.int32)


def _dispatch_kernel(fill_ref, dest_ref, h_ref, xs_hbm, zbuf, sem, zsem):
    tm = h_ref.shape[0]
    i = pl.program_id(0)

    def fill_copies(start):
        def issue(cp, on):
            @pl.when(on)
            def _():
                cp.start() if start else cp.wait()

        for e in range(MOE_EXPERTS):
            off = fill_ref[0, e]
            n_pad = fill_ref[1, e]
            bit = MOE_BLOCK // 2
            while bit:
                issue(pltpu.make_async_copy(zbuf.at[pl.ds(0, bit)], xs_hbm.at[pl.ds(off, bit)], zsem),
                      (n_pad & bit) != 0)
                off = off + (n_pad & bit)
                bit //= 2

        half = zbuf.shape[0]
        first = fill_ref[2, 0] * (MOE_BLOCK // half)

        def tail(j, c):
            cp = pltpu.make_async_copy(zbuf, xs_hbm.at[pl.ds(j * half, half)], zsem)
            cp.start() if start else cp.wait()
            return c
        lax.fori_loop(first, xs_hbm.shape[0] // half, tail, 0)

    @pl.when(i == 0)
    def _():
        zbuf[...] = jnp.zeros_like(zbuf)
        fill_copies(True)
        fill_copies(False)

    def row_copy(j, dst):
        return pltpu.make_async_copy(h_ref.at[pl.ds(j % tm, 1)], xs_hbm.at[pl.ds(dst, 1)], sem)

    for j in range(MOE_TOPK * tm):
        row_copy(j, dest_ref[0, 0, j]).start(priority=j % 2)

    def wait(j, c):
        row_copy(0, 0).wait()
        return c
    lax.fori_loop(0, MOE_TOPK * tm, wait, 0, unroll=8)


def _dispatch(h2, dest_tiles, fill, n_rows):
    t, sub, lanes = h2.shape
    nt, _, two_tm = dest_tiles.shape
    tm = two_tm // MOE_TOPK
    grid_spec = pltpu.PrefetchScalarGridSpec(
        num_scalar_prefetch=1,
        grid=(nt,),
        in_specs=[pl.BlockSpec((1, 1, two_tm), lambda i, f: (i, 0, 0), memory_space=pltpu.SMEM),
                  pl.BlockSpec((tm, sub, lanes), lambda i, f: (i, 0, 0))],
        out_specs=pl.BlockSpec(memory_space=pl.ANY),
        scratch_shapes=[pltpu.VMEM((MOE_BLOCK // 2, sub, lanes), BF16),
                        pltpu.SemaphoreType.DMA(()),
                        pltpu.SemaphoreType.DMA(())],
    )
    return pl.pallas_call(
        _dispatch_kernel,
        grid_spec=grid_spec,
        out_shape=jax.ShapeDtypeStruct((n_rows, sub, lanes), BF16),
        compiler_params=_params("arbitrary"),
        name="moe_dispatch",
    )(fill, dest_tiles, h2)


def _experts_kernel(sched_ref, nused_ref, x_ref, w1_hbm, w3_hbm, w2_hbm, y_ref,
                    w1f, w3f, w2f, w1b, w3b, w2b, wsem, *, layer):
    i = pl.program_id(0)
    used = i < nused_ref[0]
    e, first, slot, e_next = (sched_ref[r, i] for r in range(4))

    def weight_copies(ex, s):
        return [pltpu.make_async_copy(src.at[layer, ex], dst.at[s], wsem.at[s])
                for src, dst in ((w1_hbm, w1f), (w3_hbm, w3f), (w2_hbm, w2f))]

    @pl.when(i == 0)
    def _():
        for cp in weight_copies(e, 0):
            cp.start()

    @pl.when(used & (first == 1))
    def _():
        for cp in weight_copies(e, slot):
            cp.wait()

        @pl.when(e_next >= 0)
        def _():
            for cp in weight_copies(e_next, 1 - slot):
                cp.start()

        w1b[...] = w1f[slot].astype(BF16)
        w3b[...] = w3f[slot].astype(BF16)
        w2b[...] = w2f[slot].astype(BF16)

    @pl.when(used)
    def _():
        blk, sub, lanes = x_ref.shape
        x = x_ref[...].reshape(blk, sub * lanes)
        a = _dot(x, w1b[...])
        hb = (a * _sigmoid(a)) * _dot(x, w3b[...])
        y_ref[...] = _dot(hb.astype(BF16), w2b[...]).astype(BF16).reshape(blk, sub, lanes)

    @pl.when(jnp.logical_not(used))
    def _():
        y_ref[...] = jnp.zeros_like(y_ref)


def _experts(xs, sched, n_used, w1, w3, w2, layer):
    nb = sched.shape[1]
    _, sub, lanes = xs.shape
    d = sub * lanes
    hid = w1.shape[-1]
    blk = MOE_BLOCK

    def xmap(i, sc, nu):
        return (jnp.minimum(i, jnp.maximum(nu[0], 1) - 1), 0, 0)

    grid_spec = pltpu.PrefetchScalarGridSpec(
        num_scalar_prefetch=2,
        grid=(nb,),
        in_specs=[pl.BlockSpec((blk, sub, lanes), xmap),
                  pl.BlockSpec(memory_space=pl.ANY),
                  pl.BlockSpec(memory_space=pl.ANY),
                  pl.BlockSpec(memory_space=pl.ANY)],
        out_specs=pl.BlockSpec((blk, sub, lanes), lambda i, sc, nu: (i, 0, 0)),
        scratch_shapes=[pltpu.VMEM((2, d, hid), F32),
                        pltpu.VMEM((2, d, hid), F32),
                        pltpu.VMEM((2, hid, d), F32),
                        pltpu.VMEM((d, hid), BF16),
                        pltpu.VMEM((d, hid), BF16),
                        pltpu.VMEM((hid, d), BF16),
                        pltpu.SemaphoreType.DMA((2,))],
    )
    return pl.pallas_call(
        functools.partial(_experts_kernel, layer=layer),
        grid_spec=grid_spec,
        out_shape=jax.ShapeDtypeStruct((nb * blk, sub, lanes), BF16),
        compiler_params=_params("arbitrary"),
        name="moe_experts",
    )(sched, n_used, xs, w1, w3, w2)


def _combine_kernel(d0_ref, d1_ref, d2_ref, x_ref, gt_ref, gate_ref, post_ref, sh_ref, sc_ref, ys_hbm,
                    o_ref, *rest, final):
    hn_ref = None if final else rest[0]
    ybuf, sem = rest[-2:]
    tm = x_ref.shape[0] // 2
    d = x_ref.shape[1]
    n_rows = MOE_TOPK * tm
    i = pl.program_id(0)

    def row_copy(row, j, s):
        return pltpu.make_async_copy(ys_hbm.at[pl.ds(row, 1)], ybuf.at[s, pl.ds(j, 1)], sem.at[s])

    def start_rows(dst_ref, s, lo, hi):
        for j in range(lo, hi):
            row_copy(dst_ref[0, 0, j], j, s).start(priority=j % 2)

    def wait_all(s):
        def body(j, c):
            row_copy(0, 0, s).wait()
            return c
        lax.fori_loop(0, n_rows, body, 0, unroll=8)

    def tile(s, next_dst_ref):
        n_pieces = tm // COMBINE_PIECE_ROWS
        per = n_rows // n_pieces
        for p in range(n_pieces):
            start_rows(next_dst_ref, 1 - s, p * per, (p + 1) * per)
            lo = p * COMBINE_PIECE_ROWS
            rs = pl.ds(s * tm + lo, COMBINE_PIECE_ROWS)
            gt = gt_ref[rs, :]
            y0 = ybuf[s, pl.ds(lo, COMBINE_PIECE_ROWS)].reshape(COMBINE_PIECE_ROWS, d).astype(F32)
            y1 = ybuf[s, pl.ds(tm + lo, COMBINE_PIECE_ROWS)].reshape(COMBINE_PIECE_ROWS, d).astype(F32)
            x = x_ref[rs, :] + gate_ref[0] * (gt[:, 0:1] * y0 + gt[:, 1:2] * y1)
            if final:
                x = x * lax.rsqrt(jnp.mean(x * x, axis=-1, keepdims=True) + EPS) * post_ref[...]
            else:
                hn_ref[rs, :] = _norm_mod(x, post_ref[...], sh_ref[0], sc_ref[0]).astype(BF16)
            o_ref[rs, :] = x

    @pl.when(i == 0)
    def _():
        start_rows(d0_ref, 0, 0, n_rows)

    wait_all(0)
    tile(0, d1_ref)
    wait_all(1)
    tile(1, d2_ref)

    @pl.when(i == pl.num_programs(0) - 1)
    def _():
        wait_all(0)


def _combine(x, ys, dest_tiles, gates, gate, post_gain, post_shift, post_scale, rows_per_batch, final):
    t, d = x.shape
    bsz = gate.shape[0]
    nt, _, two_tm = dest_tiles.shape
    tm = two_tm // MOE_TOPK
    tpb = rows_per_batch // tm
    assert nt % 2 == 0 and tpb % 2 == 0
    smem = functools.partial(pl.BlockSpec, memory_space=pltpu.SMEM)
    per_batch = lambda i: (2 * i // tpb, 0, 0)
    rows = pl.BlockSpec((2 * tm, d), lambda i: (i, 0))
    out_specs, out_shape = [rows], [jax.ShapeDtypeStruct((t, d), F32)]
    if not final:
        out_specs.append(rows)
        out_shape.append(jax.ShapeDtypeStruct((t, d), BF16))
    return pl.pallas_call(
        functools.partial(_combine_kernel, final=final),
        grid=(nt // 2,),
        in_specs=[smem((1, 1, two_tm), lambda i: (2 * i, 0, 0)),
                  smem((1, 1, two_tm), lambda i: (2 * i + 1, 0, 0)),
                  smem((1, 1, two_tm), lambda i: ((2 * i + 2) % nt, 0, 0)),
                  rows,
                  pl.BlockSpec((2 * tm, LANES), lambda i: (i, 0)),
                  pl.BlockSpec((1, 1, d), per_batch),
                  pl.BlockSpec((1, d), lambda i: (0, 0)),
                  pl.BlockSpec((1, 1, d), per_batch),
                  pl.BlockSpec((1, 1, d), per_batch),
                  pl.BlockSpec(memory_space=pl.ANY)],
        out_specs=out_specs,
        out_shape=out_shape,
        scratch_shapes=[pltpu.VMEM((2, two_tm) + ys.shape[1:], BF16),
                        pltpu.SemaphoreType.DMA((2,))],
        compiler_params=_params("arbitrary"),
        name="moe_combine",
    )(dest_tiles, dest_tiles, dest_tiles, x, gates, gate.reshape(bsz, 1, d), post_gain.reshape(1, d),
      post_shift.reshape(bsz, 1, d), post_scale.reshape(bsz, 1, d), ys)


def _moe(x, routed, gate, w1, w3, w2, layer, post, seq, final):
    t, _ = x.shape
    tile = _pick(seq, MOE_TOKEN_TILE)
    h2, meta, gts, cnt = routed
    dest_tiles, sched, n_used, fill = _dispatch_tables(
        meta[:, :MOE_TOPK], meta[:, MOE_TOPK:2 * MOE_TOPK], cnt[0, :MOE_EXPERTS], t // tile, tile)
    xs = _dispatch(h2, dest_tiles, fill, sched.shape[1] * MOE_BLOCK)
    ys = _experts(xs, sched, n_used, w1, w3, w2, layer)
    return _combine(x, ys, dest_tiles, gts, gate, *post, seq, final)


def kernel(x, c, norm_mix, norm_ffn, ada_w, ada_b, gla_w_in, gla_b_r, gla_w_a_up, gla_b_a, gla_norm_g, gla_w_out, sgu_w_in, sgu_b_in, sgu_ln_g, sgu_ln_b, sgu_w_s, sgu_b_s, sgu_w_out, moe_w_grp, moe_b_grp, moe_w_exp, moe_b_exp, moe_w1, moe_w3, moe_w2, final_norm):
    bsz, seq, d = x.shape
    depth = norm_mix.shape[0]
    xt = x.reshape(bsz * seq, d)
    mod = _adaln(c, ada_w, ada_b)
    mods = [[mod[i, :, m * d:(m + 1) * d] for m in range(N_MOD)] for i in range(depth)]
    mixer_in = xt
    for i in range(depth):
        sh1, sc1, g1, sh2, sc2, g2 = mods[i]
        j = i // 2
        if i % 2 == 0:
            w_in = gla_w_in[j]
            n_main = w_in.shape[1] - GLA_RANK
            w_main = w_in[:, :n_main].astype(BF16)
            w_low = jnp.zeros((d, LANES), F32).at[:, :GLA_RANK].set(w_in[:, n_main:]).astype(BF16)
            proj, a_low = _norm_matmul(mixer_in, norm_mix[i], sh1, sc1, w_main, seq, "gla_in_proj",
                                       w_side=w_low, tn=2048)
            o = _gla_scan(proj, a_low, gla_w_a_up[j], gla_b_a[j], gla_b_r[j], gla_norm_g[j], bsz, seq)
            w_out, name = gla_w_out[j], "gla_out_proj"
        else:
            z, = _norm_matmul(mixer_in, norm_mix[i], sh1, sc1, sgu_w_in[j].astype(BF16), seq, "sgu_in_proj",
                              bias=sgu_b_in[j], tn=2048)
            o = _sgu_spatial(z, sgu_ln_g[j], sgu_ln_b[j], sgu_w_s[j], sgu_b_s[j])
            w_out, name = sgu_w_out[j], "sgu_out_proj"
        xt = _matmul_residual(o, w_out.astype(BF16), xt, g1, seq, name)
        routed = _router(xt, norm_ffn[i], sh2, sc2, moe_w_grp[i], moe_b_grp[i], moe_w_exp[i], moe_b_exp[i], seq)
        final = i == depth - 1
        post = (final_norm, sh2, sc2) if final else (norm_mix[i + 1], mods[i + 1][0], mods[i + 1][1])
        xt, *nxt = _moe(xt, routed, g2, moe_w1, moe_w3, moe_w2, i, post, seq, final)
        mixer_in = xt if final else nxt[0]
    return xt.reshape(bsz, seq, d)
```

```python
import functools
import math

import jax
import jax.numpy as jnp
import numpy as np
from jax import lax
from jax.experimental import pallas as pl
from jax.experimental.pallas import tpu as pltpu

F32 = jnp.float32
BF16 = jnp.bfloat16

EPS = 1e-6
N_MOD = 6
GLA_HEADS = 4
GLA_RANK = 16
GLA_TAU = 16.0
GLA_CHUNK = 256
GLA_FLAT = 8
GLA_HEADS_PER_STEP = 4
SGU_CHUNK = 128
SGU_GROUPS = 8
MOE_GROUPS = 8
MOE_PER_GROUP = 8
MOE_EXPERTS = MOE_GROUPS * MOE_PER_GROUP
MOE_TOPK = 2
MOE_BLOCK = 256
MOE_BLOCKS_PER_STEP = 2
MOE_TOKEN_TILE = 256
COMBINE_PIECE_ROWS = 16
COMBINE_TILES = 4
COMBINE_AHEAD = 2
LANES = 128
V7X_VMEM_LIMIT_BYTES = 56 * 2**20


def _params(*semantics):
    return pltpu.CompilerParams(dimension_semantics=semantics,
                                vmem_limit_bytes=V7X_VMEM_LIMIT_BYTES)


def _dot(a, b):
    return jnp.dot(a, b, preferred_element_type=F32)


def _dot_nt(a, b):
    return lax.dot_general(a, b, (((1,), (1,)), ((), ())), preferred_element_type=F32)


def _dot_tn(a, b):
    return lax.dot_general(a, b, (((0,), (0,)), ((), ())), preferred_element_type=F32)


def _split3(x):
    hi = x.astype(BF16)
    r1 = x - hi.astype(F32)
    mid = r1.astype(BF16)
    lo = (r1 - mid.astype(F32)).astype(BF16)
    return hi, mid, lo


def _sigmoid(x):
    return 1.0 / (1.0 + jnp.exp(-x))


def _gelu_tanh(x):
    c = math.sqrt(2.0 / math.pi)
    return x * (0.5 * (1.0 + jnp.tanh(c * (x + 0.044715 * (x * x * x)))))


def _pick(n, pref):
    t = min(n, pref)
    while n % t:
        t //= 2
    return t


def _adaln_kernel(c_ref, w_ref, b_ref, o_ref):
    c = c_ref[...]
    ca = c * _sigmoid(c)
    hi, mid, _ = _split3(ca)
    w = w_ref[0].astype(BF16)
    o_ref[0] = _dot(hi, w) + _dot(mid, w) + b_ref[0]


def _adaln(c, ada_w, ada_b):
    depth, d, n = ada_w.shape
    b = c.shape[0]
    rows = 8
    cp = jnp.zeros((rows, d), F32).at[:b].set(c)
    tn = _pick(n, 1024)
    out = pl.pallas_call(
        _adaln_kernel,
        grid=(depth, n // tn),
        in_specs=[pl.BlockSpec((rows, d), lambda l, j: (0, 0)),
                  pl.BlockSpec((1, d, tn), lambda l, j: (l, 0, j)),
                  pl.BlockSpec((1, 1, tn), lambda l, j: (l, 0, j))],
        out_specs=pl.BlockSpec((1, rows, tn), lambda l, j: (l, 0, j)),
        out_shape=jax.ShapeDtypeStruct((depth, rows, n), F32),
        compiler_params=_params("parallel", "parallel"),
        name="adaln",
    )(cp, ada_w, ada_b.reshape(depth, 1, n))
    return out[:, :b]


def _norm_mod(x, gain, shift, scale):
    ms = jnp.mean(x * x, axis=-1, keepdims=True)
    return (x * lax.rsqrt(ms + EPS)) * (gain * (1.0 + scale)) + shift


def _norm_matmul_kernel(x_ref, g_ref, sh_ref, sc_ref, w_ref, *rest, gelu, side, normed):
    rest = list(rest)
    bias_ref = rest.pop(0) if gelu else None
    wside_ref = rest.pop(0) if side else None
    o_ref = rest.pop(0)
    oside_ref = rest.pop(0) if side else None
    h_ref = x_ref if normed else rest.pop(0)

    if not normed or side:
        @pl.when(pl.program_id(1) == 0)
        def _():
            if not normed:
                h_ref[...] = _norm_mod(x_ref[...], g_ref[...], sh_ref[0], sc_ref[0]).astype(BF16)
            if side:
                oside_ref[...] = _dot(h_ref[...], wside_ref[...])

    acc = _dot(h_ref[...], w_ref[...])
    if gelu:
        acc = _gelu_tanh(acc + bias_ref[...])
    o_ref[...] = acc.astype(o_ref.dtype)


def _norm_matmul(x, gain, shift, scale, w, rows_per_batch, name, bias=None, w_side=None, tm=1024, tn=1024):
    normed = x.dtype == BF16
    t, d = x.shape
    n = w.shape[1]
    tm = _pick(rows_per_batch, tm)
    tn = _pick(n, tn)
    tpb = rows_per_batch // tm
    bsz = shift.shape[0]
    in_specs = [pl.BlockSpec((tm, d), lambda i, j: (i, 0)),
                pl.BlockSpec((1, d), lambda i, j: (0, 0)),
                pl.BlockSpec((1, 1, d), lambda i, j: (i // tpb, 0, 0)),
                pl.BlockSpec((1, 1, d), lambda i, j: (i // tpb, 0, 0)),
                pl.BlockSpec((d, tn), lambda i, j: (0, j))]
    args = [x, gain.reshape(1, d), shift.reshape(bsz, 1, d), scale.reshape(bsz, 1, d), w]
    out_specs = [pl.BlockSpec((tm, tn), lambda i, j: (i, j))]
    out_shape = [jax.ShapeDtypeStruct((t, n), BF16)]
    if bias is not None:
        in_specs.append(pl.BlockSpec((1, tn), lambda i, j: (0, j)))
        args.append(bias.reshape(1, n))
    if w_side is not None:
        in_specs.append(pl.BlockSpec((d, LANES), lambda i, j: (0, 0)))
        args.append(w_side)
        out_specs.append(pl.BlockSpec((tm, LANES), lambda i, j: (i, 0)))
        out_shape.append(jax.ShapeDtypeStruct((t, LANES), F32))
    return pl.pallas_call(
        functools.partial(_norm_matmul_kernel, gelu=bias is not None, side=w_side is not None, normed=normed),
        grid=(t // tm, n // tn),
        in_specs=in_specs,
        out_specs=out_specs,
        out_shape=out_shape,
        scratch_shapes=[] if normed else [pltpu.VMEM((tm, d), BF16)],
        compiler_params=_params("parallel", "arbitrary"),
        name=name,
    )(*args)


def _matmul_residual_kernel(a_ref, w_ref, x_ref, gate_ref, o_ref):
    o_ref[...] = x_ref[...] + gate_ref[0] * _dot(a_ref[...], w_ref[...])


def _matmul_residual(a, w, x, gate, rows_per_batch, name, tm=512):
    t, k = a.shape
    n = w.shape[1]
    tm = _pick(rows_per_batch, tm)
    tpb = rows_per_batch // tm
    bsz = gate.shape[0]
    return pl.pallas_call(
        _matmul_residual_kernel,
        grid=(t // tm,),
        in_specs=[pl.BlockSpec((tm, k), lambda i: (i, 0)),
                  pl.BlockSpec((k, n), lambda i: (0, 0), pipeline_mode=pl.Buffered(1)),
                  pl.BlockSpec((tm, n), lambda i: (i, 0)),
                  pl.BlockSpec((1, 1, n), lambda i: (i // tpb, 0, 0))],
        out_specs=pl.BlockSpec((tm, n), lambda i: (i, 0)),
        out_shape=jax.ShapeDtypeStruct((t, n), F32),
        compiler_params=_params("parallel"),
        name=name,
    )(a, w, x, gate.reshape(bsz, 1, n))


def _rows_bcast(b, n, off):
    c, dk = b.shape
    parts = [jnp.broadcast_to(b[j * n + off:j * n + off + 1, :], (n, dk)) for j in range(c // n)]
    return parts[0] if len(parts) == 1 else jnp.concatenate(parts, axis=0)


def _pair_levels(c):
    t = np.arange(c)[:, None]
    s = np.arange(c)[None, :]
    x = t ^ s
    lvl = np.floor(np.log2(np.maximum(x, 1))).astype(np.int32) + 1
    lvl = np.where(x < GLA_FLAT, 0, lvl)
    return np.where(s <= t, lvl, -1).astype(np.int32)


def _gla_scan_kernel(q_ref, k_ref, v_ref, r_ref, al_ref, wup_ref, ba_ref, br_ref, ng_ref, lvl_ref,
                     o_ref, st_ref, *, scale):
    c = q_ref.shape[0]
    heads, dv, dk = st_ref.shape

    @pl.when(pl.program_id(2) == 0)
    def _():
        st_ref[...] = jnp.zeros_like(st_ref)

    al = _split3(al_ref[:, :GLA_RANK])
    wu = _split3(wup_ref[...])
    pre = _dot(al[0], wu[0]) + _dot(al[0], wu[1]) + _dot(al[1], wu[0]) + ba_ref[...]
    g = (jnp.minimum(pre, 0.0) - jnp.log(1.0 + jnp.exp(-jnp.abs(pre)))) * (math.log2(math.e) / GLA_TAU)

    lvl = lvl_ref[...]
    tri = jnp.where(lvl >= 0, 1.0, 0.0).astype(BF16)
    gs = _split3(g)
    b_all = _dot(tri, gs[0]) + _dot(tri, gs[1]) + _dot(tri, gs[2])

    for hd in range(heads):
        ks = pl.ds(hd * dk, dk)
        vs = pl.ds(hd * dv, dv)
        b = b_all[:, hd * dk:(hd + 1) * dk]
        q = q_ref[:, ks] * jnp.asarray(scale, BF16)
        k = k_ref[:, ks]
        v = v_ref[:, vs]

        a = jnp.zeros((c, c), F32)
        n = c
        while n > GLA_FLAT:
            e = jnp.exp2(-jnp.abs(b - _rows_bcast(b, n, n // 2))).astype(BF16)
            a = jnp.where(lvl == n.bit_length() - 1, _dot_nt(q * e, k * e), a)
            n //= 2
        ref = _rows_bcast(b, GLA_FLAT, 0)
        qh = q * jnp.exp2(b - ref).astype(BF16)
        kh = k * jnp.exp2(ref - b).astype(BF16)
        a = jnp.where(lvl == 0, _dot_nt(qh, kh), a)

        st = st_ref[hd]
        o = _dot(a.astype(BF16), v) + _dot_nt(q * jnp.exp2(b).astype(BF16), st.astype(BF16))

        b_last = b[c - 1:c, :]
        kd = k * jnp.exp2(b_last - b).astype(BF16)
        st_ref[hd] = st * jnp.exp2(b_last) + _dot_tn(v, kd)

        on = o * lax.rsqrt(jnp.mean(o * o, axis=-1, keepdims=True) + EPS) * ng_ref[...]
        r = r_ref[:, vs].astype(F32) + br_ref[:, vs]
        o_ref[:, vs] = (r * _sigmoid(r) * on).astype(o_ref.dtype)


def _gla_scan(proj, a_low, w_a_up, b_a, b_r, norm_g, bsz, seq):
    t, n3 = proj.shape
    dkk = n3 // 6
    dvv = 2 * dkk
    h = GLA_HEADS
    dk, dv = dkk // h, dvv // h
    c = _pick(seq, GLA_CHUNK)
    nc = seq // c
    hp = GLA_HEADS_PER_STEP
    gk, gv = hp * dk, hp * dv
    voff = (2 * dkk) // gv
    roff = (2 * dkk + dvv) // gv
    rowmap = lambda b, hh, n: b * nc + n
    return pl.pallas_call(
        functools.partial(_gla_scan_kernel, scale=float(dk) ** -0.5),
        grid=(bsz, h // hp, nc),
        in_specs=[pl.BlockSpec((c, gk), lambda b, hh, n: (rowmap(b, hh, n), hh)),
                  pl.BlockSpec((c, gk), lambda b, hh, n: (rowmap(b, hh, n), h // hp + hh)),
                  pl.BlockSpec((c, gv), lambda b, hh, n: (rowmap(b, hh, n), voff + hh)),
                  pl.BlockSpec((c, gv), lambda b, hh, n: (rowmap(b, hh, n), roff + hh)),
                  pl.BlockSpec((c, LANES), lambda b, hh, n: (rowmap(b, hh, n), 0)),
                  pl.BlockSpec((GLA_RANK, gk), lambda b, hh, n: (0, hh)),
                  pl.BlockSpec((1, gk), lambda b, hh, n: (0, hh)),
                  pl.BlockSpec((1, gv), lambda b, hh, n: (0, hh)),
                  pl.BlockSpec((1, dv), lambda b, hh, n: (0, 0)),
                  pl.BlockSpec((c, c), lambda b, hh, n: (0, 0))],
        out_specs=pl.BlockSpec((c, gv), lambda b, hh, n: (rowmap(b, hh, n), hh)),
        out_shape=jax.ShapeDtypeStruct((t, dvv), BF16),
        scratch_shapes=[pltpu.VMEM((hp, dv, dk), F32)],
        compiler_params=_params("parallel", "parallel", "arbitrary"),
        name="gla_scan",
    )(proj, proj, proj, proj, a_low, w_a_up, b_a.reshape(1, dkk), b_r.reshape(1, dvv),
      norm_g.reshape(1, dv), jnp.asarray(_pair_levels(c)))


def _sgu_spatial_kernel(u_ref, v_ref, lg_ref, lb_ref, ws_ref, bs_ref, o_ref, wc_scr):
    rows, half = v_ref.shape
    c = SGU_CHUNK
    gw = half // SGU_GROUPS

    @pl.when(pl.program_id(0) == 0)
    def _():
        row = lax.broadcasted_iota(jnp.int32, (c, c), 0)
        col = lax.broadcasted_iota(jnp.int32, (c, c), 1)
        for g in range(SGU_GROUPS):
            wc_scr[g] = jnp.where(row >= col, ws_ref[g], 0.0).astype(BF16)

    for ci in range(rows // c):
        rs = pl.ds(ci * c, c)
        v = v_ref[rs, :].astype(F32)
        mu = jnp.mean(v, axis=-1, keepdims=True)
        vc = v - mu
        var = jnp.mean(vc * vc, axis=-1, keepdims=True)
        vn = (vc * lax.rsqrt(var + EPS) * lg_ref[...] + lb_ref[...]).astype(BF16)
        for g in range(SGU_GROUPS):
            cs = pl.ds(g * gw, gw)
            mixed = _dot(wc_scr[g], vn[:, g * gw:(g + 1) * gw]) + bs_ref[:, g:g + 1]
            o_ref[rs, cs] = (u_ref[rs, cs].astype(F32) * mixed).astype(o_ref.dtype)


def _sgu_spatial(z, ln_g, ln_b, w_s, b_s, rows=512):
    t, two_half = z.shape
    half = two_half // 2
    c = SGU_CHUNK
    rows = max(c, _pick(t, rows))
    return pl.pallas_call(
        _sgu_spatial_kernel,
        grid=(t // rows,),
        in_specs=[pl.BlockSpec((rows, half), lambda i: (i, 0)),
                  pl.BlockSpec((rows, half), lambda i: (i, 1)),
                  pl.BlockSpec((1, half), lambda i: (0, 0)),
                  pl.BlockSpec((1, half), lambda i: (0, 0)),
                  pl.BlockSpec((SGU_GROUPS, c, c), lambda i: (0, 0, 0)),
                  pl.BlockSpec((c, SGU_GROUPS), lambda i: (0, 0))],
        out_specs=pl.BlockSpec((rows, half), lambda i: (i, 0)),
        out_shape=jax.ShapeDtypeStruct((t, half), BF16),
        scratch_shapes=[pltpu.VMEM((SGU_GROUPS, c, c), BF16)],
        compiler_params=_params("arbitrary"),
        name="sgu_spatial",
    )(z, z, ln_g.reshape(1, half), ln_b.reshape(1, half), w_s, b_s.T)


def _route(h, w, bias, cnt):
    hi, mid, _ = _split3(h)
    l2 = _dot(hi, w) + _dot(mid, w)
    logits = l2[:, :LANES] + l2[:, LANES:] + bias
    lane = lax.broadcasted_iota(jnp.int32, logits.shape, 1)
    lane_f = lane.astype(F32)
    neg = -jnp.inf
    big = float(LANES)

    grp = jnp.where(lane < MOE_GROUPS, logits, neg)
    gmax = jnp.max(grp, axis=-1, keepdims=True)
    gidx = jnp.min(jnp.where(grp == gmax, lane_f, big), axis=-1, keepdims=True)
    g_w = 1.0 / jnp.sum(jnp.exp(grp - gmax), axis=-1, keepdims=True)

    e_lane = lane - MOE_GROUPS
    in_grp = (e_lane >= 0) & (e_lane < MOE_EXPERTS) & ((e_lane // MOE_PER_GROUP).astype(F32) == gidx)
    el = jnp.where(in_grp, logits, neg)
    m1 = jnp.max(el, axis=-1, keepdims=True)
    i1 = jnp.min(jnp.where(el == m1, lane_f, big), axis=-1, keepdims=True)
    el2 = jnp.where(lane_f == i1, neg, el)
    m2 = jnp.max(el2, axis=-1, keepdims=True)
    i2 = jnp.min(jnp.where(el2 == m2, lane_f, big), axis=-1, keepdims=True)
    z = jnp.sum(jnp.exp(el - m1), axis=-1, keepdims=True)
    p1 = 1.0 / z
    p2 = jnp.exp(m2 - m1) / z
    psum = p1 + p2
    e1 = i1 - float(MOE_GROUPS)
    e2 = i2 - float(MOE_GROUPS)

    tm = logits.shape[0]
    oh1 = (lane_f == e1).astype(F32)
    oh2 = (lane_f == e2).astype(F32)
    oh = oh1 + oh2
    row = lax.broadcasted_iota(jnp.int32, (tm, tm), 0)
    col = lax.broadcasted_iota(jnp.int32, (tm, tm), 1)
    before = _dot((row > col).astype(BF16), oh.astype(BF16)) + cnt
    r1 = jnp.sum(oh1 * before, axis=-1, keepdims=True)
    r2 = jnp.sum(oh2 * before, axis=-1, keepdims=True)
    cnt = cnt + jnp.sum(oh, axis=0, keepdims=True)

    meta = jnp.where(lane == 0, e1, jnp.where(lane == 1, e2, jnp.where(lane == 2, r1, jnp.where(lane == 3, r2, 0.0))))
    gates = jnp.where(lane == 0, g_w * (p1 / psum), jnp.where(lane == 1, g_w * (p2 / psum), 0.0))
    return hi, meta.astype(jnp.int32), gates, cnt


def _router_kernel(x_ref, g_ref, sh_ref, sc_ref, w_ref, b_ref, h_ref, id_ref, gt_ref, cnt_ref, cnt_scr):
    @pl.when(pl.program_id(0) == 0)
    def _():
        cnt_scr[...] = jnp.zeros_like(cnt_scr)

    h = _norm_mod(x_ref[...], g_ref[...], sh_ref[0], sc_ref[0])
    hi, meta, gates, cnt = _route(h, w_ref[...], b_ref[...], cnt_scr[...])
    h_ref[...] = hi.reshape(h_ref.shape)
    id_ref[...] = meta
    gt_ref[...] = gates
    cnt_scr[...] = cnt
    cnt_ref[...] = jnp.broadcast_to(cnt, cnt_ref.shape).astype(jnp.int32)


def _router(x, gain, shift, scale, w_grp, b_grp, w_exp, b_exp, rows_per_batch, tm=512):
    t, d = x.shape
    bsz = shift.shape[0]
    tm = _pick(rows_per_batch, tm)
    tpb = rows_per_batch // tm
    n_log = MOE_GROUPS + MOE_EXPERTS
    wcat = jnp.zeros((d, LANES), F32).at[:, :MOE_GROUPS].set(w_grp).at[:, MOE_GROUPS:n_log].set(w_exp)
    w_hi = wcat.astype(BF16)
    w_lo = (wcat - w_hi.astype(F32)).astype(BF16)
    w2 = jnp.concatenate([w_hi, w_lo], axis=1)
    bcat = jnp.zeros((1, LANES), F32).at[0, :MOE_GROUPS].set(b_grp).at[0, MOE_GROUPS:n_log].set(b_exp)
    return pl.pallas_call(
        _router_kernel,
        grid=(t // tm,),
        in_specs=[pl.BlockSpec((tm, d), lambda i: (i, 0)),
                  pl.BlockSpec((1, d), lambda i: (0, 0)),
                  pl.BlockSpec((1, 1, d), lambda i: (i // tpb, 0, 0)),
                  pl.BlockSpec((1, 1, d), lambda i: (i // tpb, 0, 0)),
                  pl.BlockSpec((d, 2 * LANES), lambda i: (0, 0)),
                  pl.BlockSpec((1, LANES), lambda i: (0, 0))],
        out_specs=[pl.BlockSpec((tm, d // LANES, LANES), lambda i: (i, 0, 0)),
                   pl.BlockSpec((tm, LANES), lambda i: (i, 0)),
                   pl.BlockSpec((tm, LANES), lambda i: (i, 0)),
                   pl.BlockSpec((8, LANES), lambda i: (0, 0))],
        out_shape=[jax.ShapeDtypeStruct((t, d // LANES, LANES), BF16),
                   jax.ShapeDtypeStruct((t, LANES), jnp.int32),
                   jax.ShapeDtypeStruct((t, LANES), F32),
                   jax.ShapeDtypeStruct((8, LANES), jnp.int32)],
        scratch_shapes=[pltpu.VMEM((1, LANES), F32)],
        compiler_params=_params("arbitrary"),
        name="moe_router",
    )(x, gain.reshape(1, d), shift.reshape(bsz, 1, d), scale.reshape(bsz, 1, d), w2, bcat)


def _dispatch_tables(ids, rank, counts, n_tiles, tile):
    blk = MOE_BLOCK
    a = ids.shape[0] * MOE_TOPK
    padded = (counts + blk - 1) // blk * blk
    pad_ends = jnp.cumsum(padded)
    pad_starts = pad_ends - padded
    onehot = ids[:, :, None] == jnp.arange(MOE_EXPERTS, dtype=jnp.int32)
    dest = jnp.sum(jnp.where(onehot, pad_starts, 0), axis=-1) + rank
    dest_tiles = dest.reshape(n_tiles, tile, MOE_TOPK).transpose(0, 2, 1).reshape(n_tiles, 1, MOE_TOPK * tile)
    nb = -(-(a + MOE_EXPERTS * (blk - 1)) // (blk * MOE_BLOCKS_PER_STEP)) * MOE_BLOCKS_PER_STEP
    block_start = jnp.arange(nb, dtype=jnp.int32) * blk
    block_exp = jnp.minimum(jnp.sum(pad_ends[None, :] <= block_start[:, None], axis=1), MOE_EXPERTS - 1)
    n_used = (pad_ends[-1] // blk).reshape(1)
    fill = jnp.stack([pad_starts + counts, padded - counts,
                      jnp.broadcast_to(n_used, (MOE_EXPERTS,))])
    first = jnp.concatenate([jnp.ones((1,), bool), block_exp[1:] != block_exp[:-1]])
    slot = (jnp.cumsum(first) - 1) % 2
    e_ids = jnp.arange(MOE_EXPERTS, dtype=jnp.int32)
    later = jnp.where((counts > 0)[None, :] & (e_ids[None, :] > e_ids[:, None]), e_ids[None, :], MOE_EXPERTS)
    nxt = jnp.min(later, axis=1)
    nxt = jnp.where(nxt < MOE_EXPERTS, nxt, -1)[block_exp]
    sched = jnp.stack([block_exp, first.astype(jnp.int32), slot, nxt])
    return dest_tiles.astype(jnp.int32), sched.astype(jnp.int32), n_used.astype(jnp.int32), fill.astype(jnp.int32)


def _dispatch_kernel(fill_ref, dest_ref, h_ref, xs_hbm, zbuf, sem, zsem):
    tm = h_ref.shape[0]
    i = pl.program_id(0)

    def fill_copies(start):
        def issue(cp, on):
            @pl.when(on)
            def _():
                cp.start() if start else cp.wait()

        for e in range(MOE_EXPERTS):
            off = fill_ref[0, e]
            n_pad = fill_ref[1, e]
            bit = MOE_BLOCK // 2
            while bit:
                issue(pltpu.make_async_copy(zbuf.at[pl.ds(0, bit)], xs_hbm.at[pl.ds(off, bit)], zsem),
                      (n_pad & bit) != 0)
                off = off + (n_pad & bit)
                bit //= 2

        half = zbuf.shape[0]
        first = fill_ref[2, 0] * (MOE_BLOCK // half)

        def tail(j, c):
            cp = pltpu.make_async_copy(zbuf, xs_hbm.at[pl.ds(j * half, half)], zsem)
            cp.start() if start else cp.wait()
            return c
        lax.fori_loop(first, xs_hbm.shape[0] // half, tail, 0)

    @pl.when(i == 0)
    def _():
        zbuf[...] = jnp.zeros_like(zbuf)
        fill_copies(True)
        fill_copies(False)

    def row_copy(j, dst):
        return pltpu.make_async_copy(h_ref.at[pl.ds(j % tm, 1)], xs_hbm.at[pl.ds(dst, 1)], sem)

    for j in range(MOE_TOPK * tm):
        row_copy(j, dest_ref[0, 0, j]).start(priority=j % 2)

    def wait(j, c):
        row_copy(0, 0).wait()
        return c
    lax.fori_loop(0, MOE_TOPK * tm, wait, 0, unroll=8)


def _dispatch(h2, dest_tiles, fill, n_rows):
    t, sub, lanes = h2.shape
    nt, _, two_tm = dest_tiles.shape
    tm = two_tm // MOE_TOPK
    grid_spec = pltpu.PrefetchScalarGridSpec(
        num_scalar_prefetch=1,
        grid=(nt,),
        in_specs=[pl.BlockSpec((1, 1, two_tm), lambda i, f: (i, 0, 0), memory_space=pltpu.SMEM),
                  pl.BlockSpec((tm, sub, lanes), lambda i, f: (i, 0, 0))],
        out_specs=pl.BlockSpec(memory_space=pl.ANY),
        scratch_shapes=[pltpu.VMEM((MOE_BLOCK // 2, sub, lanes), BF16),
                        pltpu.SemaphoreType.DMA(()),
                        pltpu.SemaphoreType.DMA(())],
    )
    return pl.pallas_call(
        _dispatch_kernel,
        grid_spec=grid_spec,
        out_shape=jax.ShapeDtypeStruct((n_rows, sub, lanes), BF16),
        compiler_params=_params("arbitrary"),
        name="moe_dispatch",
    )(fill, dest_tiles, h2)


def _experts_kernel(sched_ref, nused_ref, x_ref, w1_hbm, w3_hbm, w2_hbm, y_ref,
                    w1f, w3f, w2f, w1b, w3b, w2b, wsem, *, layer):
    blk = MOE_BLOCK
    _, sub, lanes = x_ref.shape

    def weight_copies(ex, s):
        return [pltpu.make_async_copy(src.at[layer, ex], dst.at[s], wsem.at[s])
                for src, dst in ((w1_hbm, w1f), (w3_hbm, w3f), (w2_hbm, w2f))]

    @pl.when(pl.program_id(0) == 0)
    def _():
        for cp in weight_copies(sched_ref[0, 0], 0):
            cp.start()

    for piece in range(x_ref.shape[0] // blk):
        b = pl.program_id(0) * (x_ref.shape[0] // blk) + piece
        rows = pl.ds(piece * blk, blk)
        used = b < nused_ref[0]
        e, first, slot, e_next = (sched_ref[r, b] for r in range(4))

        @pl.when(used & (first == 1))
        def _():
            for cp in weight_copies(e, slot):
                cp.wait()

            @pl.when(e_next >= 0)
            def _():
                for cp in weight_copies(e_next, 1 - slot):
                    cp.start()

            w1b[...] = w1f[slot].astype(BF16)
            w3b[...] = w3f[slot].astype(BF16)
            w2b[...] = w2f[slot].astype(BF16)

        @pl.when(used)
        def _():
            x = x_ref[rows].reshape(blk, sub * lanes)
            a = _dot(x, w1b[...])
            hb = (a * _sigmoid(a)) * _dot(x, w3b[...])
            y_ref[rows] = _dot(hb.astype(BF16), w2b[...]).astype(BF16).reshape(blk, sub, lanes)

        @pl.when(jnp.logical_not(used))
        def _():
            y_ref[rows] = jnp.zeros((blk, sub, lanes), y_ref.dtype)


def _experts(xs, sched, n_used, w1, w3, w2, layer):
    nb = sched.shape[1]
    _, sub, lanes = xs.shape
    d = sub * lanes
    hid = w1.shape[-1]
    blk = MOE_BLOCK
    per = MOE_BLOCKS_PER_STEP
    assert nb % per == 0

    def xmap(i, sc, nu):
        return (jnp.minimum(i, (jnp.maximum(nu[0], 1) + per - 1) // per - 1), 0, 0)

    grid_spec = pltpu.PrefetchScalarGridSpec(
        num_scalar_prefetch=2,
        grid=(nb // per,),
        in_specs=[pl.BlockSpec((per * blk, sub, lanes), xmap),
                  pl.BlockSpec(memory_space=pl.ANY),
                  pl.BlockSpec(memory_space=pl.ANY),
                  pl.BlockSpec(memory_space=pl.ANY)],
        out_specs=pl.BlockSpec((per * blk, sub, lanes), lambda i, sc, nu: (i, 0, 0)),
        scratch_shapes=[pltpu.VMEM((2, d, hid), F32),
                        pltpu.VMEM((2, d, hid), F32),
                        pltpu.VMEM((2, hid, d), F32),
                        pltpu.VMEM((d, hid), BF16),
                        pltpu.VMEM((d, hid), BF16),
                        pltpu.VMEM((hid, d), BF16),
                        pltpu.SemaphoreType.DMA((2,))],
    )
    return pl.pallas_call(
        functools.partial(_experts_kernel, layer=layer),
        grid_spec=grid_spec,
        out_shape=jax.ShapeDtypeStruct((nb * blk, sub, lanes), BF16),
        compiler_params=_params("arbitrary"),
        name="moe_experts",
    )(sched, n_used, xs, w1, w3, w2)


def _combine_kernel(*refs, final):
    nd = COMBINE_TILES + COMBINE_AHEAD
    dst_refs = refs[:nd]
    x_ref, gt_ref, gate_ref, post_ref, sh_ref, sc_ref, ys_hbm, o_ref = refs[nd:nd + 8]
    rest = refs[nd + 8:]
    hn_ref = None if final else rest[0]
    ybuf, sem = rest[-2:]
    tm = x_ref.shape[0] // COMBINE_TILES
    d = x_ref.shape[1]
    n_rows = MOE_TOPK * tm
    i = pl.program_id(0)

    def row_copy(row, j, s):
        return pltpu.make_async_copy(ys_hbm.at[pl.ds(row, 1)], ybuf.at[s, pl.ds(j, 1)], sem.at[s])

    def start_rows(dst_ref, s, lo, hi):
        for j in range(lo, hi):
            row_copy(dst_ref[0, 0, j], j, s).start(priority=j % 2)

    def wait_all(s):
        def body(j, c):
            row_copy(0, 0, s).wait()
            return c
        lax.fori_loop(0, n_rows, body, 0, unroll=8)

    def tile(s):
        n_pieces = tm // COMBINE_PIECE_ROWS
        per = n_rows // n_pieces
        for p in range(n_pieces):
            start_rows(dst_refs[s + COMBINE_AHEAD], (s + COMBINE_AHEAD) % COMBINE_TILES, p * per, (p + 1) * per)
            lo = p * COMBINE_PIECE_ROWS
            rs = pl.ds(s * tm + lo, COMBINE_PIECE_ROWS)
            gt = gt_ref[rs, :]
            y0 = ybuf[s, pl.ds(lo, COMBINE_PIECE_ROWS)].reshape(COMBINE_PIECE_ROWS, d).astype(F32)
            y1 = ybuf[s, pl.ds(tm + lo, COMBINE_PIECE_ROWS)].reshape(COMBINE_PIECE_ROWS, d).astype(F32)
            x = x_ref[rs, :] + gate_ref[0] * (gt[:, 0:1] * y0 + gt[:, 1:2] * y1)
            if final:
                x = x * lax.rsqrt(jnp.mean(x * x, axis=-1, keepdims=True) + EPS) * post_ref[...]
            else:
                hn_ref[rs, :] = _norm_mod(x, post_ref[...], sh_ref[0], sc_ref[0]).astype(BF16)
            o_ref[rs, :] = x

    @pl.when(i == 0)
    def _():
        for s in range(COMBINE_AHEAD):
            start_rows(dst_refs[s], s, 0, n_rows)

    for s in range(COMBINE_TILES):
        wait_all(s)
        tile(s)

    @pl.when(i == pl.num_programs(0) - 1)
    def _():
        for s in range(COMBINE_AHEAD):
            wait_all(s)


def _combine(x, ys, dest_tiles, gates, gate, post_gain, post_shift, post_scale, rows_per_batch, final):
    t, d = x.shape
    bsz = gate.shape[0]
    nt, _, two_tm = dest_tiles.shape
    tm = two_tm // MOE_TOPK
    tpb = rows_per_batch // tm
    k = COMBINE_TILES
    assert nt % k == 0 and tpb % k == 0 and COMBINE_AHEAD < k
    smem = functools.partial(pl.BlockSpec, memory_space=pltpu.SMEM)
    per_batch = lambda i: (k * i // tpb, 0, 0)
    rows = pl.BlockSpec((k * tm, d), lambda i: (i, 0))
    out_specs, out_shape = [rows], [jax.ShapeDtypeStruct((t, d), F32)]
    if not final:
        out_specs.append(rows)
        out_shape.append(jax.ShapeDtypeStruct((t, d), BF16))
    n_dst = k + COMBINE_AHEAD
    return pl.pallas_call(
        functools.partial(_combine_kernel, final=final),
        grid=(nt // k,),
        in_specs=[smem((1, 1, two_tm), lambda i, s=s: ((k * i + s) % nt, 0, 0)) for s in range(n_dst)] + [
                  rows,
                  pl.BlockSpec((k * tm, LANES), lambda i: (i, 0)),
                  pl.BlockSpec((1, 1, d), per_batch),
                  pl.BlockSpec((1, d), lambda i: (0, 0)),
                  pl.BlockSpec((1, 1, d), per_batch),
                  pl.BlockSpec((1, 1, d), per_batch),
                  pl.BlockSpec(memory_space=pl.ANY)],
        out_specs=out_specs,
        out_shape=out_shape,
        scratch_shapes=[pltpu.VMEM((k, two_tm) + ys.shape[1:], BF16),
                        pltpu.SemaphoreType.DMA((k,))],
        compiler_params=_params("arbitrary"),
        name="moe_combine",
    )(*([dest_tiles] * n_dst), x, gates, gate.reshape(bsz, 1, d), post_gain.reshape(1, d),
      post_shift.reshape(bsz, 1, d), post_scale.reshape(bsz, 1, d), ys)


def _moe(x, routed, gate, w1, w3, w2, layer, post, seq, final):
    t, _ = x.shape
    tile = _pick(seq, MOE_TOKEN_TILE)
    h2, meta, gts, cnt = routed
    dest_tiles, sched, n_used, fill = _dispatch_tables(
        meta[:, :MOE_TOPK], meta[:, MOE_TOPK:2 * MOE_TOPK], cnt[0, :MOE_EXPERTS], t // tile, tile)
    xs = _dispatch(h2, dest_tiles, fill, sched.shape[1] * MOE_BLOCK)
    ys = _experts(xs, sched, n_used, w1, w3, w2, layer)
    return _combine(x, ys, dest_tiles, gts, gate, *post, seq, final)


def kernel(x, c, norm_mix, norm_ffn, ada_w, ada_b, gla_w_in, gla_b_r, gla_w_a_up, gla_b_a, gla_norm_g, gla_w_out, sgu_w_in, sgu_b_in, sgu_ln_g, sgu_ln_b, sgu_w_s, sgu_b_s, sgu_w_out, moe_w_grp, moe_b_grp, moe_w_exp, moe_b_exp, moe_w1, moe_w3, moe_w2, final_norm):
    bsz, seq, d = x.shape
    depth = norm_mix.shape[0]
    xt = x.reshape(bsz * seq, d)
    mod = _adaln(c, ada_w, ada_b)
    mods = [[mod[i, :, m * d:(m + 1) * d] for m in range(N_MOD)] for i in range(depth)]
    mixer_in = xt
    for i in range(depth):
        sh1, sc1, g1, sh2, sc2, g2 = mods[i]
        j = i // 2
        if i % 2 == 0:
            w_in = gla_w_in[j]
            n_main = w_in.shape[1] - GLA_RANK
            w_main = w_in[:, :n_main].astype(BF16)
            w_low = jnp.zeros((d, LANES), F32).at[:, :GLA_RANK].set(w_in[:, n_main:]).astype(BF16)
            proj, a_low = _norm_matmul(mixer_in, norm_mix[i], sh1, sc1, w_main, seq, "gla_in_proj",
                                       w_side=w_low, tn=2048)
            o = _gla_scan(proj, a_low, gla_w_a_up[j], gla_b_a[j], gla_b_r[j], gla_norm_g[j], bsz, seq)
            xt = _matmul_residual(o, gla_w_out[j].astype(BF16), xt, g1, seq, "gla_out_proj")
        else:
            z, = _norm_matmul(mixer_in, norm_mix[i], sh1, sc1, sgu_w_in[j].astype(BF16), seq, "sgu_in_proj",
                              bias=sgu_b_in[j], tn=2048)
            o = _sgu_spatial(z, sgu_ln_g[j], sgu_ln_b[j], sgu_w_s[j], sgu_b_s[j])
            xt = _matmul_residual(o, sgu_w_out[j].astype(BF16), xt, g1, seq, "sgu_out_proj")
        routed = _router(xt, norm_ffn[i], sh2, sc2, moe_w_grp[i], moe_b_grp[i], moe_w_exp[i], moe_b_exp[i], seq)
        final = i == depth - 1
        post = (final_norm, sh2, sc2) if final else (norm_mix[i + 1], mods[i + 1][0], mods[i + 1][1])
        xt, *nxt = _moe(xt, routed, g2, moe_w1, moe_w3, moe_w2, i, post, seq, final)
        mixer_in = xt if final else nxt[0]
    return xt.reshape(bsz, seq, d)
```

```python
import functools
import math

import jax
import jax.numpy as jnp
import numpy as np
from jax import lax
from jax.experimental import pallas as pl
from jax.experimental.pallas import tpu as pltpu

F32 = jnp.float32
BF16 = jnp.bfloat16

EPS = 1e-6
N_MOD = 6
GLA_HEADS = 4
GLA_RANK = 16
GLA_TAU = 16.0
GLA_CHUNK = 256
GLA_FLAT = 8
GLA_HEADS_PER_STEP = 4
SGU_CHUNK = 128
SGU_GROUPS = 8
MOE_GROUPS = 8
MOE_PER_GROUP = 8
MOE_EXPERTS = MOE_GROUPS * MOE_PER_GROUP
MOE_TOPK = 2
MOE_BLOCK = 256
MOE_BLOCKS_PER_STEP = 2
MOE_TOKEN_TILE = 256
COMBINE_PIECE_ROWS = 16
COMBINE_TILES = 4
COMBINE_AHEAD = 2
LANES = 128
V7X_VMEM_LIMIT_BYTES = 56 * 2**20


def _params(*semantics):
    return pltpu.CompilerParams(dimension_semantics=semantics,
                                vmem_limit_bytes=V7X_VMEM_LIMIT_BYTES)


def _dot(a, b):
    return jnp.dot(a, b, preferred_element_type=F32)


def _dot_nt(a, b):
    return lax.dot_general(a, b, (((1,), (1,)), ((), ())), preferred_element_type=F32)


def _dot_tn(a, b):
    return lax.dot_general(a, b, (((0,), (0,)), ((), ())), preferred_element_type=F32)


def _split3(x):
    hi = x.astype(BF16)
    r1 = x - hi.astype(F32)
    mid = r1.astype(BF16)
    lo = (r1 - mid.astype(F32)).astype(BF16)
    return hi, mid, lo


def _sigmoid(x):
    return 1.0 / (1.0 + jnp.exp(-x))


def _gelu_tanh(x):
    c = math.sqrt(2.0 / math.pi)
    return x * (0.5 * (1.0 + jnp.tanh(c * (x + 0.044715 * (x * x * x)))))


def _pick(n, pref):
    t = min(n, pref)
    while n % t:
        t //= 2
    return t


def _adaln_kernel(c_ref, w_ref, b_ref, o_ref):
    c = c_ref[...]
    ca = c * _sigmoid(c)
    hi, mid, _ = _split3(ca)
    w = w_ref[0].astype(BF16)
    o_ref[0] = _dot(hi, w) + _dot(mid, w) + b_ref[0]


def _adaln(c, ada_w, ada_b):
    depth, d, n = ada_w.shape
    b = c.shape[0]
    rows = 8
    cp = jnp.zeros((rows, d), F32).at[:b].set(c)
    tn = _pick(n, 1024)
    out = pl.pallas_call(
        _adaln_kernel,
        grid=(depth, n // tn),
        in_specs=[pl.BlockSpec((rows, d), lambda l, j: (0, 0)),
                  pl.BlockSpec((1, d, tn), lambda l, j: (l, 0, j)),
                  pl.BlockSpec((1, 1, tn), lambda l, j: (l, 0, j))],
        out_specs=pl.BlockSpec((1, rows, tn), lambda l, j: (l, 0, j)),
        out_shape=jax.ShapeDtypeStruct((depth, rows, n), F32),
        compiler_params=_params("parallel", "parallel"),
        name="adaln",
    )(cp, ada_w, ada_b.reshape(depth, 1, n))
    return out[:, :b]


def _norm_mod(x, gain, shift, scale):
    ms = jnp.mean(x * x, axis=-1, keepdims=True)
    return (x * lax.rsqrt(ms + EPS)) * (gain * (1.0 + scale)) + shift


def _norm_matmul_kernel(x_ref, g_ref, sh_ref, sc_ref, w_ref, *rest, gelu, side, normed):
    rest = list(rest)
    bias_ref = rest.pop(0) if gelu else None
    wside_ref = rest.pop(0) if side else None
    o_ref = rest.pop(0)
    oside_ref = rest.pop(0) if side else None
    h_ref = x_ref if normed else rest.pop(0)

    if not normed or side:
        @pl.when(pl.program_id(1) == 0)
        def _():
            if not normed:
                h_ref[...] = _norm_mod(x_ref[...], g_ref[...], sh_ref[0], sc_ref[0]).astype(BF16)
            if side:
                oside_ref[...] = _dot(h_ref[...], wside_ref[...])

    acc = _dot(h_ref[...], w_ref[...])
    if gelu:
        acc = _gelu_tanh((acc + bias_ref[...]).astype(BF16))
    o_ref[...] = acc.astype(o_ref.dtype)


def _norm_matmul(x, gain, shift, scale, w, rows_per_batch, name, bias=None, w_side=None, tm=1024, tn=1024):
    normed = x.dtype == BF16
    t, d = x.shape
    n = w.shape[1]
    tm = _pick(rows_per_batch, tm)
    tn = _pick(n, tn)
    tpb = rows_per_batch // tm
    bsz = shift.shape[0]
    in_specs = [pl.BlockSpec((tm, d), lambda i, j: (i, 0)),
                pl.BlockSpec((1, d), lambda i, j: (0, 0)),
                pl.BlockSpec((1, 1, d), lambda i, j: (i // tpb, 0, 0)),
                pl.BlockSpec((1, 1, d), lambda i, j: (i // tpb, 0, 0)),
                pl.BlockSpec((d, tn), lambda i, j: (0, j))]
    args = [x, gain.reshape(1, d), shift.reshape(bsz, 1, d), scale.reshape(bsz, 1, d), w]
    out_specs = [pl.BlockSpec((tm, tn), lambda i, j: (i, j))]
    out_shape = [jax.ShapeDtypeStruct((t, n), BF16)]
    if bias is not None:
        in_specs.append(pl.BlockSpec((1, tn), lambda i, j: (0, j)))
        args.append(bias.reshape(1, n))
    if w_side is not None:
        in_specs.append(pl.BlockSpec((d, LANES), lambda i, j: (0, 0)))
        args.append(w_side)
        out_specs.append(pl.BlockSpec((tm, LANES), lambda i, j: (i, 0)))
        out_shape.append(jax.ShapeDtypeStruct((t, LANES), F32))
    return pl.pallas_call(
        functools.partial(_norm_matmul_kernel, gelu=bias is not None, side=w_side is not None, normed=normed),
        grid=(t // tm, n // tn),
        in_specs=in_specs,
        out_specs=out_specs,
        out_shape=out_shape,
        scratch_shapes=[] if normed else [pltpu.VMEM((tm, d), BF16)],
        compiler_params=_params("parallel", "arbitrary"),
        name=name,
    )(*args)


def _matmul_residual_kernel(a_ref, w_ref, x_ref, gate_ref, o_ref):
    o_ref[...] = x_ref[...] + gate_ref[0] * _dot(a_ref[...], w_ref[...])


def _matmul_residual(a, w, x, gate, rows_per_batch, name, tm=512):
    t, k = a.shape
    n = w.shape[1]
    tm = _pick(rows_per_batch, tm)
    tpb = rows_per_batch // tm
    bsz = gate.shape[0]
    return pl.pallas_call(
        _matmul_residual_kernel,
        grid=(t // tm,),
        in_specs=[pl.BlockSpec((tm, k), lambda i: (i, 0)),
                  pl.BlockSpec((k, n), lambda i: (0, 0), pipeline_mode=pl.Buffered(1)),
                  pl.BlockSpec((tm, n), lambda i: (i, 0)),
                  pl.BlockSpec((1, 1, n), lambda i: (i // tpb, 0, 0))],
        out_specs=pl.BlockSpec((tm, n), lambda i: (i, 0)),
        out_shape=jax.ShapeDtypeStruct((t, n), F32),
        compiler_params=_params("parallel"),
        name=name,
    )(a, w, x, gate.reshape(bsz, 1, n))


def _rows_bcast(b, n, off):
    c, dk = b.shape
    parts = [jnp.broadcast_to(b[j * n + off:j * n + off + 1, :], (n, dk)) for j in range(c // n)]
    return parts[0] if len(parts) == 1 else jnp.concatenate(parts, axis=0)


def _pair_levels(c):
    t = np.arange(c)[:, None]
    s = np.arange(c)[None, :]
    x = t ^ s
    lvl = np.floor(np.log2(np.maximum(x, 1))).astype(np.int32) + 1
    lvl = np.where(x < GLA_FLAT, 0, lvl)
    return np.where(s <= t, lvl, -1).astype(np.int32)


def _gla_scan_kernel(q_ref, k_ref, v_ref, r_ref, al_ref, wup_ref, ba_ref, br_ref, ng_ref, lvl_ref,
                     o_ref, st_ref, *, scale):
    c = q_ref.shape[0]
    heads, dv, dk = st_ref.shape

    @pl.when(pl.program_id(2) == 0)
    def _():
        st_ref[...] = jnp.zeros_like(st_ref)

    al = _split3(al_ref[:, :GLA_RANK])
    wu = _split3(wup_ref[...])
    pre = _dot(al[0], wu[0]) + _dot(al[0], wu[1]) + _dot(al[1], wu[0]) + ba_ref[...]
    g = (jnp.minimum(pre, 0.0) - jnp.log(1.0 + jnp.exp(-jnp.abs(pre)))) * (math.log2(math.e) / GLA_TAU)

    lvl = lvl_ref[...]
    tri = jnp.where(lvl >= 0, 1.0, 0.0).astype(BF16)
    gs = _split3(g)
    b_all = _dot(tri, gs[0]) + _dot(tri, gs[1]) + _dot(tri, gs[2])

    for hd in range(heads):
        ks = pl.ds(hd * dk, dk)
        vs = pl.ds(hd * dv, dv)
        b = b_all[:, hd * dk:(hd + 1) * dk]
        q = q_ref[:, ks] * jnp.asarray(scale, BF16)
        k = k_ref[:, ks]
        v = v_ref[:, vs]

        a = jnp.zeros((c, c), F32)
        n = c
        while n > GLA_FLAT:
            e = jnp.exp2(-jnp.abs(b - _rows_bcast(b, n, n // 2))).astype(BF16)
            a = jnp.where(lvl == n.bit_length() - 1, _dot_nt(q * e, k * e), a)
            n //= 2
        ref = _rows_bcast(b, GLA_FLAT, 0)
        qh = q * jnp.exp2(b - ref).astype(BF16)
        kh = k * jnp.exp2(ref - b).astype(BF16)
        a = jnp.where(lvl == 0, _dot_nt(qh, kh), a)

        st = st_ref[hd]
        o = _dot(a.astype(BF16), v) + _dot_nt(q * jnp.exp2(b).astype(BF16), st.astype(BF16))

        b_last = b[c - 1:c, :]
        kd = k * jnp.exp2(b_last - b).astype(BF16)
        st_ref[hd] = st * jnp.exp2(b_last) + _dot_tn(v, kd)

        on = o * lax.rsqrt(jnp.mean(o * o, axis=-1, keepdims=True) + EPS) * ng_ref[...]
        r = r_ref[:, vs].astype(F32) + br_ref[:, vs]
        o_ref[:, vs] = (r * _sigmoid(r) * on).astype(o_ref.dtype)


def _gla_scan(proj, a_low, w_a_up, b_a, b_r, norm_g, bsz, seq):
    t, n3 = proj.shape
    dkk = n3 // 6
    dvv = 2 * dkk
    h = GLA_HEADS
    dk, dv = dkk // h, dvv // h
    c = _pick(seq, GLA_CHUNK)
    nc = seq // c
    hp = GLA_HEADS_PER_STEP
    gk, gv = hp * dk, hp * dv
    voff = (2 * dkk) // gv
    roff = (2 * dkk + dvv) // gv
    rowmap = lambda b, hh, n: b * nc + n
    return pl.pallas_call(
        functools.partial(_gla_scan_kernel, scale=float(dk) ** -0.5),
        grid=(bsz, h // hp, nc),
        in_specs=[pl.BlockSpec((c, gk), lambda b, hh, n: (rowmap(b, hh, n), hh)),
                  pl.BlockSpec((c, gk), lambda b, hh, n: (rowmap(b, hh, n), h // hp + hh)),
                  pl.BlockSpec((c, gv), lambda b, hh, n: (rowmap(b, hh, n), voff + hh)),
                  pl.BlockSpec((c, gv), lambda b, hh, n: (rowmap(b, hh, n), roff + hh)),
                  pl.BlockSpec((c, LANES), lambda b, hh, n: (rowmap(b, hh, n), 0)),
                  pl.BlockSpec((GLA_RANK, gk), lambda b, hh, n: (0, hh)),
                  pl.BlockSpec((1, gk), lambda b, hh, n: (0, hh)),
                  pl.BlockSpec((1, gv), lambda b, hh, n: (0, hh)),
                  pl.BlockSpec((1, dv), lambda b, hh, n: (0, 0)),
                  pl.BlockSpec((c, c), lambda b, hh, n: (0, 0))],
        out_specs=pl.BlockSpec((c, gv), lambda b, hh, n: (rowmap(b, hh, n), hh)),
        out_shape=jax.ShapeDtypeStruct((t, dvv), BF16),
        scratch_shapes=[pltpu.VMEM((hp, dv, dk), F32)],
        compiler_params=_params("parallel", "parallel", "arbitrary"),
        name="gla_scan",
    )(proj, proj, proj, proj, a_low, w_a_up, b_a.reshape(1, dkk), b_r.reshape(1, dvv),
      norm_g.reshape(1, dv), jnp.asarray(_pair_levels(c)))


def _sgu_spatial_kernel(u_ref, v_ref, lg_ref, lb_ref, ws_ref, bs_ref, o_ref, wc_scr):
    rows, half = v_ref.shape
    c = SGU_CHUNK
    gw = half // SGU_GROUPS

    @pl.when(pl.program_id(0) == 0)
    def _():
        row = lax.broadcasted_iota(jnp.int32, (c, c), 0)
        col = lax.broadcasted_iota(jnp.int32, (c, c), 1)
        for g in range(SGU_GROUPS):
            wc_scr[g] = jnp.where(row >= col, ws_ref[g], 0.0).astype(BF16)

    for ci in range(rows // c):
        rs = pl.ds(ci * c, c)
        v = v_ref[rs, :].astype(F32)
        mu = jnp.mean(v, axis=-1, keepdims=True)
        vc = v - mu
        var = jnp.mean(vc * vc, axis=-1, keepdims=True)
        vn = (vc * lax.rsqrt(var + EPS) * lg_ref[...] + lb_ref[...]).astype(BF16)
        for g in range(SGU_GROUPS):
            cs = pl.ds(g * gw, gw)
            mixed = _dot(wc_scr[g], vn[:, g * gw:(g + 1) * gw]) + bs_ref[:, g:g + 1]
            o_ref[rs, cs] = u_ref[rs, cs] * mixed.astype(BF16)


def _sgu_spatial(z, ln_g, ln_b, w_s, b_s, rows=512):
    t, two_half = z.shape
    half = two_half // 2
    c = SGU_CHUNK
    rows = max(c, _pick(t, rows))
    return pl.pallas_call(
        _sgu_spatial_kernel,
        grid=(t // rows,),
        in_specs=[pl.BlockSpec((rows, half), lambda i: (i, 0)),
                  pl.BlockSpec((rows, half), lambda i: (i, 1)),
                  pl.BlockSpec((1, half), lambda i: (0, 0)),
                  pl.BlockSpec((1, half), lambda i: (0, 0)),
                  pl.BlockSpec((SGU_GROUPS, c, c), lambda i: (0, 0, 0)),
                  pl.BlockSpec((c, SGU_GROUPS), lambda i: (0, 0))],
        out_specs=pl.BlockSpec((rows, half), lambda i: (i, 0)),
        out_shape=jax.ShapeDtypeStruct((t, half), BF16),
        scratch_shapes=[pltpu.VMEM((SGU_GROUPS, c, c), BF16)],
        compiler_params=_params("arbitrary"),
        name="sgu_spatial",
    )(z, z, ln_g.reshape(1, half), ln_b.reshape(1, half), w_s, b_s.T)


def _route(h, w, bias, cnt):
    hi, mid, _ = _split3(h)
    l2 = _dot(hi, w) + _dot(mid, w)
    logits = l2[:, :LANES] + l2[:, LANES:] + bias
    lane = lax.broadcasted_iota(jnp.int32, logits.shape, 1)
    lane_f = lane.astype(F32)
    neg = -jnp.inf
    big = float(LANES)

    grp = jnp.where(lane < MOE_GROUPS, logits, neg)
    gmax = jnp.max(grp, axis=-1, keepdims=True)
    gidx = jnp.min(jnp.where(grp == gmax, lane_f, big), axis=-1, keepdims=True)
    g_w = 1.0 / jnp.sum(jnp.exp(grp - gmax), axis=-1, keepdims=True)

    e_lane = lane - MOE_GROUPS
    in_grp = (e_lane >= 0) & (e_lane < MOE_EXPERTS) & ((e_lane // MOE_PER_GROUP).astype(F32) == gidx)
    el = jnp.where(in_grp, logits, neg)
    m1 = jnp.max(el, axis=-1, keepdims=True)
    i1 = jnp.min(jnp.where(el == m1, lane_f, big), axis=-1, keepdims=True)
    el2 = jnp.where(lane_f == i1, neg, el)
    m2 = jnp.max(el2, axis=-1, keepdims=True)
    i2 = jnp.min(jnp.where(el2 == m2, lane_f, big), axis=-1, keepdims=True)
    z = jnp.sum(jnp.exp(el - m1), axis=-1, keepdims=True)
    p1 = 1.0 / z
    p2 = jnp.exp(m2 - m1) / z
    psum = p1 + p2
    e1 = i1 - float(MOE_GROUPS)
    e2 = i2 - float(MOE_GROUPS)

    tm = logits.shape[0]
    oh1 = (lane_f == e1).astype(F32)
    oh2 = (lane_f == e2).astype(F32)
    oh = oh1 + oh2
    row = lax.broadcasted_iota(jnp.int32, (tm, tm), 0)
    col = lax.broadcasted_iota(jnp.int32, (tm, tm), 1)
    before = _dot((row > col).astype(BF16), oh.astype(BF16)) + cnt
    r1 = jnp.sum(oh1 * before, axis=-1, keepdims=True)
    r2 = jnp.sum(oh2 * before, axis=-1, keepdims=True)
    cnt = cnt + jnp.sum(oh, axis=0, keepdims=True)

    meta = jnp.where(lane == 0, e1, jnp.where(lane == 1, e2, jnp.where(lane == 2, r1, jnp.where(lane == 3, r2, 0.0))))
    gates = jnp.where(lane == 0, g_w * (p1 / psum), jnp.where(lane == 1, g_w * (p2 / psum), 0.0))
    return hi, meta.astype(jnp.int32), gates, cnt


def _router_kernel(x_ref, g_ref, sh_ref, sc_ref, w_ref, b_ref, h_ref, id_ref, gt_ref, cnt_ref, cnt_scr):
    @pl.when(pl.program_id(0) == 0)
    def _():
        cnt_scr[...] = jnp.zeros_like(cnt_scr)

    h = _norm_mod(x_ref[...], g_ref[...], sh_ref[0], sc_ref[0])
    hi, meta, gates, cnt = _route(h, w_ref[...], b_ref[...], cnt_scr[...])
    h_ref[...] = hi.reshape(h_ref.shape)
    id_ref[...] = meta
    gt_ref[...] = gates
    cnt_scr[...] = cnt
    cnt_ref[...] = jnp.broadcast_to(cnt, cnt_ref.shape).astype(jnp.int32)


def _router(x, gain, shift, scale, w_grp, b_grp, w_exp, b_exp, rows_per_batch, tm=512):
    t, d = x.shape
    bsz = shift.shape[0]
    tm = _pick(rows_per_batch, tm)
    tpb = rows_per_batch // tm
    n_log = MOE_GROUPS + MOE_EXPERTS
    wcat = jnp.zeros((d, LANES), F32).at[:, :MOE_GROUPS].set(w_grp).at[:, MOE_GROUPS:n_log].set(w_exp)
    w_hi = wcat.astype(BF16)
    w_lo = (wcat - w_hi.astype(F32)).astype(BF16)
    w2 = jnp.concatenate([w_hi, w_lo], axis=1)
    bcat = jnp.zeros((1, LANES), F32).at[0, :MOE_GROUPS].set(b_grp).at[0, MOE_GROUPS:n_log].set(b_exp)
    return pl.pallas_call(
        _router_kernel,
        grid=(t // tm,),
        in_specs=[pl.BlockSpec((tm, d), lambda i: (i, 0)),
                  pl.BlockSpec((1, d), lambda i: (0, 0)),
                  pl.BlockSpec((1, 1, d), lambda i: (i // tpb, 0, 0)),
                  pl.BlockSpec((1, 1, d), lambda i: (i // tpb, 0, 0)),
                  pl.BlockSpec((d, 2 * LANES), lambda i: (0, 0)),
                  pl.BlockSpec((1, LANES), lambda i: (0, 0))],
        out_specs=[pl.BlockSpec((tm, d // LANES, LANES), lambda i: (i, 0, 0)),
                   pl.BlockSpec((tm, LANES), lambda i: (i, 0)),
                   pl.BlockSpec((tm, LANES), lambda i: (i, 0)),
                   pl.BlockSpec((8, LANES), lambda i: (0, 0))],
        out_shape=[jax.ShapeDtypeStruct((t, d // LANES, LANES), BF16),
                   jax.ShapeDtypeStruct((t, LANES), jnp.int32),
                   jax.ShapeDtypeStruct((t, LANES), F32),
                   jax.ShapeDtypeStruct((8, LANES), jnp.int32)],
        scratch_shapes=[pltpu.VMEM((1, LANES), F32)],
        compiler_params=_params("arbitrary"),
        name="moe_router",
    )(x, gain.reshape(1, d), shift.reshape(bsz, 1, d), scale.reshape(bsz, 1, d), w2, bcat)


def _dispatch_tables(ids, rank, counts, n_tiles, tile):
    blk = MOE_BLOCK
    a = ids.shape[0] * MOE_TOPK
    padded = (counts + blk - 1) // blk * blk
    pad_ends = jnp.cumsum(padded)
    pad_starts = pad_ends - padded
    onehot = ids[:, :, None] == jnp.arange(MOE_EXPERTS, dtype=jnp.int32)
    dest = jnp.sum(jnp.where(onehot, pad_starts, 0), axis=-1) + rank
    dest_tiles = dest.reshape(n_tiles, tile, MOE_TOPK).transpose(0, 2, 1).reshape(n_tiles, 1, MOE_TOPK * tile)
    nb = -(-(a + MOE_EXPERTS * (blk - 1)) // (blk * MOE_BLOCKS_PER_STEP)) * MOE_BLOCKS_PER_STEP
    block_start = jnp.arange(nb, dtype=jnp.int32) * blk
    block_exp = jnp.minimum(jnp.sum(pad_ends[None, :] <= block_start[:, None], axis=1), MOE_EXPERTS - 1)
    n_used = (pad_ends[-1] // blk).reshape(1)
    fill = jnp.stack([pad_starts + counts, padded - counts,
                      jnp.broadcast_to(n_used, (MOE_EXPERTS,))])
    first = jnp.concatenate([jnp.ones((1,), bool), block_exp[1:] != block_exp[:-1]])
    slot = (jnp.cumsum(first) - 1) % 2
    e_ids = jnp.arange(MOE_EXPERTS, dtype=jnp.int32)
    later = jnp.where((counts > 0)[None, :] & (e_ids[None, :] > e_ids[:, None]), e_ids[None, :], MOE_EXPERTS)
    nxt = jnp.min(later, axis=1)
    nxt = jnp.where(nxt < MOE_EXPERTS, nxt, -1)[block_exp]
    sched = jnp.stack([block_exp, first.astype(jnp.int32), slot, nxt])
    return dest_tiles.astype(jnp.int32), sched.astype(jnp.int32), n_used.astype(jnp.int32), fill.astype(jnp.int32)


def _dispatch_kernel(fill_ref, dest_ref, h_ref, xs_hbm, zbuf, sem, zsem):
    tm = h_ref.shape[0]
    i = pl.program_id(0)

    def fill_copies(start):
        def issue(cp, on):
            @pl.when(on)
            def _():
                cp.start() if start else cp.wait()

        for e in range(MOE_EXPERTS):
            off = fill_ref[0, e]
            n_pad = fill_ref[1, e]
            bit = MOE_BLOCK // 2
            while bit:
                issue(pltpu.make_async_copy(zbuf.at[pl.ds(0, bit)], xs_hbm.at[pl.ds(off, bit)], zsem),
                      (n_pad & bit) != 0)
                off = off + (n_pad & bit)
                bit //= 2

        half = zbuf.shape[0]
        first = fill_ref[2, 0] * (MOE_BLOCK // half)

        def tail(j, c):
            cp = pltpu.make_async_copy(zbuf, xs_hbm.at[pl.ds(j * half, half)], zsem)
            cp.start() if start else cp.wait()
            return c
        lax.fori_loop(first, xs_hbm.shape[0] // half, tail, 0)

    @pl.when(i == 0)
    def _():
        zbuf[...] = jnp.zeros_like(zbuf)
        fill_copies(True)
        fill_copies(False)

    def row_copy(j, dst):
        return pltpu.make_async_copy(h_ref.at[pl.ds(j % tm, 1)], xs_hbm.at[pl.ds(dst, 1)], sem)

    for j in range(MOE_TOPK * tm):
        row_copy(j, dest_ref[0, 0, j]).start(priority=j % 2)

    def wait(j, c):
        row_copy(0, 0).wait()
        return c
    lax.fori_loop(0, MOE_TOPK * tm, wait, 0, unroll=8)


def _dispatch(h2, dest_tiles, fill, n_rows):
    t, sub, lanes = h2.shape
    nt, _, two_tm = dest_tiles.shape
    tm = two_tm // MOE_TOPK
    grid_spec = pltpu.PrefetchScalarGridSpec(
        num_scalar_prefetch=1,
        grid=(nt,),
        in_specs=[pl.BlockSpec((1, 1, two_tm), lambda i, f: (i, 0, 0), memory_space=pltpu.SMEM),
                  pl.BlockSpec((tm, sub, lanes), lambda i, f: (i, 0, 0))],
        out_specs=pl.BlockSpec(memory_space=pl.ANY),
        scratch_shapes=[pltpu.VMEM((MOE_BLOCK // 2, sub, lanes), BF16),
                        pltpu.SemaphoreType.DMA(()),
                        pltpu.SemaphoreType.DMA(())],
    )
    return pl.pallas_call(
        _dispatch_kernel,
        grid_spec=grid_spec,
        out_shape=jax.ShapeDtypeStruct((n_rows, sub, lanes), BF16),
        compiler_params=_params("arbitrary"),
        name="moe_dispatch",
    )(fill, dest_tiles, h2)


def _experts_kernel(sched_ref, nused_ref, x_ref, w1_hbm, w3_hbm, w2_hbm, y_ref,
                    w1f, w3f, w2f, w1b, w3b, w2b, wsem, *, layer):
    blk = MOE_BLOCK
    _, sub, lanes = x_ref.shape

    def weight_copies(ex, s):
        return [pltpu.make_async_copy(src.at[layer, ex], dst.at[s], wsem.at[s])
                for src, dst in ((w1_hbm, w1f), (w3_hbm, w3f), (w2_hbm, w2f))]

    @pl.when(pl.program_id(0) == 0)
    def _():
        for cp in weight_copies(sched_ref[0, 0], 0):
            cp.start()

    for piece in range(x_ref.shape[0] // blk):
        b = pl.program_id(0) * (x_ref.shape[0] // blk) + piece
        rows = pl.ds(piece * blk, blk)
        used = b < nused_ref[0]
        e, first, slot, e_next = (sched_ref[r, b] for r in range(4))

        @pl.when(used & (first == 1))
        def _():
            for cp in weight_copies(e, slot):
                cp.wait()

            @pl.when(e_next >= 0)
            def _():
                for cp in weight_copies(e_next, 1 - slot):
                    cp.start()

            w1b[...] = w1f[slot].astype(BF16)
            w3b[...] = w3f[slot].astype(BF16)
            w2b[...] = w2f[slot].astype(BF16)

        @pl.when(used)
        def _():
            x = x_ref[rows].reshape(blk, sub * lanes)
            a = _dot(x, w1b[...])
            hb = (a * _sigmoid(a)) * _dot(x, w3b[...])
            y_ref[rows] = _dot(hb.astype(BF16), w2b[...]).astype(BF16).reshape(blk, sub, lanes)

        @pl.when(jnp.logical_not(used))
        def _():
            y_ref[rows] = jnp.zeros((blk, sub, lanes), y_ref.dtype)


def _experts(xs, sched, n_used, w1, w3, w2, layer):
    nb = sched.shape[1]
    _, sub, lanes = xs.shape
    d = sub * lanes
    hid = w1.shape[-1]
    blk = MOE_BLOCK
    per = MOE_BLOCKS_PER_STEP
    assert nb % per == 0

    def xmap(i, sc, nu):
        return (jnp.minimum(i, (jnp.maximum(nu[0], 1) + per - 1) // per - 1), 0, 0)

    grid_spec = pltpu.PrefetchScalarGridSpec(
        num_scalar_prefetch=2,
        grid=(nb // per,),
        in_specs=[pl.BlockSpec((per * blk, sub, lanes), xmap),
                  pl.BlockSpec(memory_space=pl.ANY),
                  pl.BlockSpec(memory_space=pl.ANY),
                  pl.BlockSpec(memory_space=pl.ANY)],
        out_specs=pl.BlockSpec((per * blk, sub, lanes), lambda i, sc, nu: (i, 0, 0)),
        scratch_shapes=[pltpu.VMEM((2, d, hid), F32),
                        pltpu.VMEM((2, d, hid), F32),
                        pltpu.VMEM((2, hid, d), F32),
                        pltpu.VMEM((d, hid), BF16),
                        pltpu.VMEM((d, hid), BF16),
                        pltpu.VMEM((hid, d), BF16),
                        pltpu.SemaphoreType.DMA((2,))],
    )
    return pl.pallas_call(
        functools.partial(_experts_kernel, layer=layer),
        grid_spec=grid_spec,
        out_shape=jax.ShapeDtypeStruct((nb * blk, sub, lanes), BF16),
        compiler_params=_params("arbitrary"),
        name="moe_experts",
    )(sched, n_used, xs, w1, w3, w2)


def _combine_kernel(*refs, final):
    nd = COMBINE_TILES + COMBINE_AHEAD
    dst_refs = refs[:nd]
    x_ref, gt_ref, gate_ref, post_ref, sh_ref, sc_ref, ys_hbm, o_ref = refs[nd:nd + 8]
    rest = refs[nd + 8:]
    hn_ref = None if final else rest[0]
    ybuf, sem = rest[-2:]
    tm = x_ref.shape[0] // COMBINE_TILES
    d = x_ref.shape[1]
    n_rows = MOE_TOPK * tm
    i = pl.program_id(0)

    def row_copy(row, j, s):
        return pltpu.make_async_copy(ys_hbm.at[pl.ds(row, 1)], ybuf.at[s, pl.ds(j, 1)], sem.at[s])

    def start_rows(dst_ref, s, lo, hi):
        for j in range(lo, hi):
            row_copy(dst_ref[0, 0, j], j, s).start(priority=j % 2)

    def wait_all(s):
        def body(j, c):
            row_copy(0, 0, s).wait()
            return c
        lax.fori_loop(0, n_rows, body, 0, unroll=8)

    def tile(s):
        n_pieces = tm // COMBINE_PIECE_ROWS
        per = n_rows // n_pieces
        for p in range(n_pieces):
            start_rows(dst_refs[s + COMBINE_AHEAD], (s + COMBINE_AHEAD) % COMBINE_TILES, p * per, (p + 1) * per)
            lo = p * COMBINE_PIECE_ROWS
            rs = pl.ds(s * tm + lo, COMBINE_PIECE_ROWS)
            gt = gt_ref[rs, :]
            y0 = ybuf[s, pl.ds(lo, COMBINE_PIECE_ROWS)].reshape(COMBINE_PIECE_ROWS, d).astype(F32)
            y1 = ybuf[s, pl.ds(tm + lo, COMBINE_PIECE_ROWS)].reshape(COMBINE_PIECE_ROWS, d).astype(F32)
            x = x_ref[rs, :] + gate_ref[0] * (gt[:, 0:1] * y0 + gt[:, 1:2] * y1)
            if final:
                x = x * lax.rsqrt(jnp.mean(x * x, axis=-1, keepdims=True) + EPS) * post_ref[...]
            else:
                hn_ref[rs, :] = _norm_mod(x, post_ref[...], sh_ref[0], sc_ref[0]).astype(BF16)
            o_ref[rs, :] = x

    @pl.when(i == 0)
    def _():
        for s in range(COMBINE_AHEAD):
            start_rows(dst_refs[s], s, 0, n_rows)

    for s in range(COMBINE_TILES):
        wait_all(s)
        tile(s)

    @pl.when(i == pl.num_programs(0) - 1)
    def _():
        for s in range(COMBINE_AHEAD):
            wait_all(s)


def _combine(x, ys, dest_tiles, gates, gate, post_gain, post_shift, post_scale, rows_per_batch, final):
    t, d = x.shape
    bsz = gate.shape[0]
    nt, _, two_tm = dest_tiles.shape
    tm = two_tm // MOE_TOPK
    tpb = rows_per_batch // tm
    k = COMBINE_TILES
    assert nt % k == 0 and tpb % k == 0 and COMBINE_AHEAD < k
    smem = functools.partial(pl.BlockSpec, memory_space=pltpu.SMEM)
    per_batch = lambda i: (k * i // tpb, 0, 0)
    rows = pl.BlockSpec((k * tm, d), lambda i: (i, 0))
    out_specs, out_shape = [rows], [jax.ShapeDtypeStruct((t, d), F32)]
    if not final:
        out_specs.append(rows)
        out_shape.append(jax.ShapeDtypeStruct((t, d), BF16))
    n_dst = k + COMBINE_AHEAD
    return pl.pallas_call(
        functools.partial(_combine_kernel, final=final),
        grid=(nt // k,),
        in_specs=[smem((1, 1, two_tm), lambda i, s=s: ((k * i + s) % nt, 0, 0)) for s in range(n_dst)] + [
                  rows,
                  pl.BlockSpec((k * tm, LANES), lambda i: (i, 0)),
                  pl.BlockSpec((1, 1, d), per_batch),
                  pl.BlockSpec((1, d), lambda i: (0, 0)),
                  pl.BlockSpec((1, 1, d), per_batch),
                  pl.BlockSpec((1, 1, d), per_batch),
                  pl.BlockSpec(memory_space=pl.ANY)],
        out_specs=out_specs,
        out_shape=out_shape,
        scratch_shapes=[pltpu.VMEM((k, two_tm) + ys.shape[1:], BF16),
                        pltpu.SemaphoreType.DMA((k,))],
        compiler_params=_params("arbitrary"),
        name="moe_combine",
    )(*([dest_tiles] * n_dst), x, gates, gate.reshape(bsz, 1, d), post_gain.reshape(1, d),
      post_shift.reshape(bsz, 1, d), post_scale.reshape(bsz, 1, d), ys)


def _moe(x, routed, gate, w1, w3, w2, layer, post, seq, final):
    t, _ = x.shape
    tile = _pick(seq, MOE_TOKEN_TILE)
    h2, meta, gts, cnt = routed
    dest_tiles, sched, n_used, fill = _dispatch_tables(
        meta[:, :MOE_TOPK], meta[:, MOE_TOPK:2 * MOE_TOPK], cnt[0, :MOE_EXPERTS], t // tile, tile)
    xs = _dispatch(h2, dest_tiles, fill, sched.shape[1] * MOE_BLOCK)
    ys = _experts(xs, sched, n_used, w1, w3, w2, layer)
    return _combine(x, ys, dest_tiles, gts, gate, *post, seq, final)


def kernel(x, c, norm_mix, norm_ffn, ada_w, ada_b, gla_w_in, gla_b_r, gla_w_a_up, gla_b_a, gla_norm_g, gla_w_out, sgu_w_in, sgu_b_in, sgu_ln_g, sgu_ln_b, sgu_w_s, sgu_b_s, sgu_w_out, moe_w_grp, moe_b_grp, moe_w_exp, moe_b_exp, moe_w1, moe_w3, moe_w2, final_norm):
    bsz, seq, d = x.shape
    depth = norm_mix.shape[0]
    xt = x.reshape(bsz * seq, d)
    mod = _adaln(c, ada_w, ada_b)
    mods = [[mod[i, :, m * d:(m + 1) * d] for m in range(N_MOD)] for i in range(depth)]
    mixer_in = xt
    for i in range(depth):
        sh1, sc1, g1, sh2, sc2, g2 = mods[i]
        j = i // 2
        if i % 2 == 0:
            w_in = gla_w_in[j]
            n_main = w_in.shape[1] - GLA_RANK
            w_main = w_in[:, :n_main].astype(BF16)
            w_low = jnp.zeros((d, LANES), F32).at[:, :GLA_RANK].set(w_in[:, n_main:]).astype(BF16)
            proj, a_low = _norm_matmul(mixer_in, norm_mix[i], sh1, sc1, w_main, seq, "gla_in_proj",
                                       w_side=w_low, tn=2048)
            o = _gla_scan(proj, a_low, gla_w_a_up[j], gla_b_a[j], gla_b_r[j], gla_norm_g[j], bsz, seq)
            xt = _matmul_residual(o, gla_w_out[j].astype(BF16), xt, g1, seq, "gla_out_proj")
        else:
            z, = _norm_matmul(mixer_in, norm_mix[i], sh1, sc1, sgu_w_in[j].astype(BF16), seq, "sgu_in_proj",
                              bias=sgu_b_in[j], tm=2048, tn=1024)
            o = _sgu_spatial(z, sgu_ln_g[j], sgu_ln_b[j], sgu_w_s[j], sgu_b_s[j])
            xt = _matmul_residual(o, sgu_w_out[j].astype(BF16), xt, g1, seq, "sgu_out_proj")
        routed = _router(xt, norm_ffn[i], sh2, sc2, moe_w_grp[i], moe_b_grp[i], moe_w_exp[i], moe_b_exp[i], seq)
        final = i == depth - 1
        post = (final_norm, sh2, sc2) if final else (norm_mix[i + 1], mods[i + 1][0], mods[i + 1][1])
        xt, *nxt = _moe(xt, routed, g2, moe_w1, moe_w3, moe_w2, i, post, seq, final)
        mixer_in = xt if final else nxt[0]
    return xt.reshape(bsz, seq, d)
```

```python
import functools
import math

import jax
import jax.numpy as jnp
import numpy as np
from jax import lax
from jax.experimental import pallas as pl
from jax.experimental.pallas import tpu as pltpu

F32 = jnp.float32
BF16 = jnp.bfloat16

EPS = 1e-6
N_MOD = 6
GLA_HEADS = 4
GLA_RANK = 16
GLA_TAU = 16.0
GLA_CHUNK = 256
GLA_FLAT = 8
GLA_HEADS_PER_STEP = 4
SGU_CHUNK = 128
SGU_GROUPS = 8
MOE_GROUPS = 8
MOE_PER_GROUP = 8
MOE_EXPERTS = MOE_GROUPS * MOE_PER_GROUP
MOE_TOPK = 2
MOE_BLOCK = 256
MOE_BLOCKS_PER_STEP = 2
MOE_TOKEN_TILE = 256
MOE_DISPATCH_TILE = 1024
COMBINE_PIECE_ROWS = 32
COMBINE_TILES = 4
COMBINE_AHEAD = 2
LANES = 128
V7X_VMEM_LIMIT_BYTES = 56 * 2**20


def _params(*semantics):
    return pltpu.CompilerParams(dimension_semantics=semantics,
                                vmem_limit_bytes=V7X_VMEM_LIMIT_BYTES)


def _dot(a, b):
    return jnp.dot(a, b, preferred_element_type=F32)


def _dot_nt(a, b):
    return lax.dot_general(a, b, (((1,), (1,)), ((), ())), preferred_element_type=F32)


def _dot_tn(a, b):
    return lax.dot_general(a, b, (((0,), (0,)), ((), ())), preferred_element_type=F32)


def _split3(x):
    hi = x.astype(BF16)
    r1 = x - hi.astype(F32)
    mid = r1.astype(BF16)
    lo = (r1 - mid.astype(F32)).astype(BF16)
    return hi, mid, lo


def _sigmoid(x):
    return 1.0 / (1.0 + jnp.exp(-x))


def _gelu_tanh(x):
    c = math.sqrt(2.0 / math.pi)
    return x * (0.5 * (1.0 + jnp.tanh(c * (x + 0.044715 * (x * x * x)))))


def _pick(n, pref):
    t = min(n, pref)
    while n % t:
        t //= 2
    return t


def _adaln_kernel(c_ref, w_ref, b_ref, o_ref):
    c = c_ref[...]
    ca = c * _sigmoid(c)
    hi, mid, _ = _split3(ca)
    w = w_ref[0].astype(BF16)
    o_ref[0] = _dot(hi, w) + _dot(mid, w) + b_ref[0]


def _adaln(c, ada_w, ada_b):
    depth, d, n = ada_w.shape
    b = c.shape[0]
    rows = 8
    cp = jnp.zeros((rows, d), F32).at[:b].set(c)
    tn = _pick(n, 1024)
    out = pl.pallas_call(
        _adaln_kernel,
        grid=(depth, n // tn),
        in_specs=[pl.BlockSpec((rows, d), lambda l, j: (0, 0)),
                  pl.BlockSpec((1, d, tn), lambda l, j: (l, 0, j)),
                  pl.BlockSpec((1, 1, tn), lambda l, j: (l, 0, j))],
        out_specs=pl.BlockSpec((1, rows, tn), lambda l, j: (l, 0, j)),
        out_shape=jax.ShapeDtypeStruct((depth, rows, n), F32),
        compiler_params=_params("parallel", "parallel"),
        name="adaln",
    )(cp, ada_w, ada_b.reshape(depth, 1, n))
    return out[:, :b]


def _norm_mod(x, gain, shift, scale):
    ms = jnp.mean(x * x, axis=-1, keepdims=True)
    return (x * lax.rsqrt(ms + EPS)) * (gain * (1.0 + scale)) + shift


def _norm_matmul_kernel(x_ref, g_ref, sh_ref, sc_ref, w_ref, *rest, gelu, side, normed):
    rest = list(rest)
    bias_ref = rest.pop(0) if gelu else None
    wside_ref = rest.pop(0) if side else None
    o_ref = rest.pop(0)
    oside_ref = rest.pop(0) if side else None
    h_ref = x_ref if normed else rest.pop(0)

    if not normed or side:
        @pl.when(pl.program_id(1) == 0)
        def _():
            if not normed:
                h_ref[...] = _norm_mod(x_ref[...], g_ref[...], sh_ref[0], sc_ref[0]).astype(BF16)
            if side:
                oside_ref[...] = _dot(h_ref[...], wside_ref[...])

    acc = _dot(h_ref[...], w_ref[...])
    if gelu:
        acc = _gelu_tanh((acc + bias_ref[...]).astype(BF16))
    o_ref[...] = acc.astype(o_ref.dtype)


def _norm_matmul(x, gain, shift, scale, w, rows_per_batch, name, bias=None, w_side=None, tm=1024, tn=1024):
    normed = x.dtype == BF16
    t, d = x.shape
    n = w.shape[1]
    tm = _pick(rows_per_batch, tm)
    tn = _pick(n, tn)
    tpb = rows_per_batch // tm
    bsz = shift.shape[0]
    in_specs = [pl.BlockSpec((tm, d), lambda i, j: (i, 0)),
                pl.BlockSpec((1, d), lambda i, j: (0, 0)),
                pl.BlockSpec((1, 1, d), lambda i, j: (i // tpb, 0, 0)),
                pl.BlockSpec((1, 1, d), lambda i, j: (i // tpb, 0, 0)),
                pl.BlockSpec((d, tn), lambda i, j: (0, j))]
    args = [x, gain.reshape(1, d), shift.reshape(bsz, 1, d), scale.reshape(bsz, 1, d), w]
    out_specs = [pl.BlockSpec((tm, tn), lambda i, j: (i, j))]
    out_shape = [jax.ShapeDtypeStruct((t, n), BF16)]
    if bias is not None:
        in_specs.append(pl.BlockSpec((1, tn), lambda i, j: (0, j)))
        args.append(bias.reshape(1, n))
    if w_side is not None:
        in_specs.append(pl.BlockSpec((d, LANES), lambda i, j: (0, 0)))
        args.append(w_side)
        out_specs.append(pl.BlockSpec((tm, LANES), lambda i, j: (i, 0)))
        out_shape.append(jax.ShapeDtypeStruct((t, LANES), F32))
    return pl.pallas_call(
        functools.partial(_norm_matmul_kernel, gelu=bias is not None, side=w_side is not None, normed=normed),
        grid=(t // tm, n // tn),
        in_specs=in_specs,
        out_specs=out_specs,
        out_shape=out_shape,
        scratch_shapes=[] if normed else [pltpu.VMEM((tm, d), BF16)],
        compiler_params=_params("parallel", "arbitrary"),
        name=name,
    )(*args)


def _matmul_residual_kernel(a_ref, w_ref, x_ref, gate_ref, o_ref):
    o_ref[...] = x_ref[...] + gate_ref[0] * _dot(a_ref[...], w_ref[...])


def _matmul_residual(a, w, x, gate, rows_per_batch, name, tm=512):
    t, k = a.shape
    n = w.shape[1]
    tm = _pick(rows_per_batch, tm)
    tpb = rows_per_batch // tm
    bsz = gate.shape[0]
    return pl.pallas_call(
        _matmul_residual_kernel,
        grid=(t // tm,),
        in_specs=[pl.BlockSpec((tm, k), lambda i: (i, 0)),
                  pl.BlockSpec((k, n), lambda i: (0, 0), pipeline_mode=pl.Buffered(1)),
                  pl.BlockSpec((tm, n), lambda i: (i, 0)),
                  pl.BlockSpec((1, 1, n), lambda i: (i // tpb, 0, 0))],
        out_specs=pl.BlockSpec((tm, n), lambda i: (i, 0)),
        out_shape=jax.ShapeDtypeStruct((t, n), F32),
        compiler_params=_params("parallel"),
        name=name,
    )(a, w, x, gate.reshape(bsz, 1, n))


def _rows_bcast(b, n, off):
    c, dk = b.shape
    parts = [jnp.broadcast_to(b[j * n + off:j * n + off + 1, :], (n, dk)) for j in range(c // n)]
    return parts[0] if len(parts) == 1 else jnp.concatenate(parts, axis=0)


def _pair_levels(c):
    t = np.arange(c)[:, None]
    s = np.arange(c)[None, :]
    x = t ^ s
    lvl = np.floor(np.log2(np.maximum(x, 1))).astype(np.int32) + 1
    lvl = np.where(x < GLA_FLAT, 0, lvl)
    return np.where(s <= t, lvl, -1).astype(np.int32)


def _gla_scan_kernel(q_ref, k_ref, v_ref, r_ref, al_ref, wup_ref, ba_ref, br_ref, ng_ref, lvl_ref,
                     o_ref, st_ref, *, scale):
    c = q_ref.shape[0]
    heads, dv, dk = st_ref.shape

    @pl.when(pl.program_id(2) == 0)
    def _():
        st_ref[...] = jnp.zeros_like(st_ref)

    al = _split3(al_ref[:, :GLA_RANK])
    wu = _split3(wup_ref[...])
    pre = _dot(al[0], wu[0]) + _dot(al[0], wu[1]) + _dot(al[1], wu[0]) + ba_ref[...]
    g = (jnp.minimum(pre, 0.0) - jnp.log(1.0 + jnp.exp(-jnp.abs(pre)))) * (math.log2(math.e) / GLA_TAU)

    lvl = lvl_ref[...]
    tri = jnp.where(lvl >= 0, 1.0, 0.0).astype(BF16)
    gs = _split3(g)
    b_all = _dot(tri, gs[0]) + _dot(tri, gs[1]) + _dot(tri, gs[2])

    for hd in range(heads):
        ks = pl.ds(hd * dk, dk)
        vs = pl.ds(hd * dv, dv)
        b = b_all[:, hd * dk:(hd + 1) * dk]
        q = q_ref[:, ks] * jnp.asarray(scale, BF16)
        k = k_ref[:, ks]
        v = v_ref[:, vs]

        a = jnp.zeros((c, c), F32)
        n = c
        while n > GLA_FLAT:
            e = jnp.exp2(-jnp.abs(b - _rows_bcast(b, n, n // 2))).astype(BF16)
            a = jnp.where(lvl == n.bit_length() - 1, _dot_nt(q * e, k * e), a)
            n //= 2
        ref = _rows_bcast(b, GLA_FLAT, 0)
        qh = q * jnp.exp2(b - ref).astype(BF16)
        kh = k * jnp.exp2(ref - b).astype(BF16)
        a = jnp.where(lvl == 0, _dot_nt(qh, kh), a)

        st = st_ref[hd]
        o = _dot(a.astype(BF16), v) + _dot_nt(q * jnp.exp2(b).astype(BF16), st.astype(BF16))

        b_last = b[c - 1:c, :]
        kd = k * jnp.exp2(b_last - b).astype(BF16)
        st_ref[hd] = st * jnp.exp2(b_last) + _dot_tn(v, kd)

        on = o * lax.rsqrt(jnp.mean(o * o, axis=-1, keepdims=True) + EPS) * ng_ref[...]
        r = r_ref[:, vs].astype(F32) + br_ref[:, vs]
        o_ref[:, vs] = (r * _sigmoid(r) * on).astype(o_ref.dtype)


def _gla_scan(proj, a_low, w_a_up, b_a, b_r, norm_g, bsz, seq):
    t, n3 = proj.shape
    dkk = n3 // 6
    dvv = 2 * dkk
    h = GLA_HEADS
    dk, dv = dkk // h, dvv // h
    c = _pick(seq, GLA_CHUNK)
    nc = seq // c
    hp = GLA_HEADS_PER_STEP
    gk, gv = hp * dk, hp * dv
    voff = (2 * dkk) // gv
    roff = (2 * dkk + dvv) // gv
    rowmap = lambda b, hh, n: b * nc + n
    return pl.pallas_call(
        functools.partial(_gla_scan_kernel, scale=float(dk) ** -0.5),
        grid=(bsz, h // hp, nc),
        in_specs=[pl.BlockSpec((c, gk), lambda b, hh, n: (rowmap(b, hh, n), hh)),
                  pl.BlockSpec((c, gk), lambda b, hh, n: (rowmap(b, hh, n), h // hp + hh)),
                  pl.BlockSpec((c, gv), lambda b, hh, n: (rowmap(b, hh, n), voff + hh)),
                  pl.BlockSpec((c, gv), lambda b, hh, n: (rowmap(b, hh, n), roff + hh)),
                  pl.BlockSpec((c, LANES), lambda b, hh, n: (rowmap(b, hh, n), 0)),
                  pl.BlockSpec((GLA_RANK, gk), lambda b, hh, n: (0, hh)),
                  pl.BlockSpec((1, gk), lambda b, hh, n: (0, hh)),
                  pl.BlockSpec((1, gv), lambda b, hh, n: (0, hh)),
                  pl.BlockSpec((1, dv), lambda b, hh, n: (0, 0)),
                  pl.BlockSpec((c, c), lambda b, hh, n: (0, 0))],
        out_specs=pl.BlockSpec((c, gv), lambda b, hh, n: (rowmap(b, hh, n), hh)),
        out_shape=jax.ShapeDtypeStruct((t, dvv), BF16),
        scratch_shapes=[pltpu.VMEM((hp, dv, dk), F32)],
        compiler_params=_params("parallel", "parallel", "arbitrary"),
        name="gla_scan",
    )(proj, proj, proj, proj, a_low, w_a_up, b_a.reshape(1, dkk), b_r.reshape(1, dvv),
      norm_g.reshape(1, dv), jnp.asarray(_pair_levels(c)))


def _sgu_spatial_kernel(u_ref, v_ref, lg_ref, lb_ref, ws_ref, bs_ref, o_ref, wc_scr):
    rows, half = v_ref.shape
    c = SGU_CHUNK
    gw = half // SGU_GROUPS

    @pl.when(pl.program_id(0) == 0)
    def _():
        row = lax.broadcasted_iota(jnp.int32, (c, c), 0)
        col = lax.broadcasted_iota(jnp.int32, (c, c), 1)
        for g in range(SGU_GROUPS):
            wc_scr[g] = jnp.where(row >= col, ws_ref[g], 0.0).astype(BF16)

    for ci in range(rows // c):
        rs = pl.ds(ci * c, c)
        v = v_ref[rs, :].astype(F32)
        mu = jnp.mean(v, axis=-1, keepdims=True)
        vc = v - mu
        var = jnp.mean(vc * vc, axis=-1, keepdims=True)
        vn = (vc * lax.rsqrt(var + EPS) * lg_ref[...] + lb_ref[...]).astype(BF16)
        for g in range(SGU_GROUPS):
            cs = pl.ds(g * gw, gw)
            mixed = _dot(wc_scr[g], vn[:, g * gw:(g + 1) * gw]) + bs_ref[:, g:g + 1]
            o_ref[rs, cs] = u_ref[rs, cs] * mixed.astype(BF16)


def _sgu_spatial(z, ln_g, ln_b, w_s, b_s, rows=512):
    t, two_half = z.shape
    half = two_half // 2
    c = SGU_CHUNK
    rows = max(c, _pick(t, rows))
    return pl.pallas_call(
        _sgu_spatial_kernel,
        grid=(t // rows,),
        in_specs=[pl.BlockSpec((rows, half), lambda i: (i, 0)),
                  pl.BlockSpec((rows, half), lambda i: (i, 1)),
                  pl.BlockSpec((1, half), lambda i: (0, 0)),
                  pl.BlockSpec((1, half), lambda i: (0, 0)),
                  pl.BlockSpec((SGU_GROUPS, c, c), lambda i: (0, 0, 0)),
                  pl.BlockSpec((c, SGU_GROUPS), lambda i: (0, 0))],
        out_specs=pl.BlockSpec((rows, half), lambda i: (i, 0)),
        out_shape=jax.ShapeDtypeStruct((t, half), BF16),
        scratch_shapes=[pltpu.VMEM((SGU_GROUPS, c, c), BF16)],
        compiler_params=_params("arbitrary"),
        name="sgu_spatial",
    )(z, z, ln_g.reshape(1, half), ln_b.reshape(1, half), w_s, b_s.T)


def _route(h, w, bias, cnt):
    hi, mid, _ = _split3(h)
    l2 = _dot(hi, w) + _dot(mid, w)
    logits = l2[:, :LANES] + l2[:, LANES:] + bias
    lane = lax.broadcasted_iota(jnp.int32, logits.shape, 1)
    lane_f = lane.astype(F32)
    neg = -jnp.inf
    big = float(LANES)

    grp = jnp.where(lane < MOE_GROUPS, logits, neg)
    gmax = jnp.max(grp, axis=-1, keepdims=True)
    gidx = jnp.min(jnp.where(grp == gmax, lane_f, big), axis=-1, keepdims=True)
    g_w = 1.0 / jnp.sum(jnp.exp(grp - gmax), axis=-1, keepdims=True)

    e_lane = lane - MOE_GROUPS
    in_grp = (e_lane >= 0) & (e_lane < MOE_EXPERTS) & ((e_lane // MOE_PER_GROUP).astype(F32) == gidx)
    el = jnp.where(in_grp, logits, neg)
    m1 = jnp.max(el, axis=-1, keepdims=True)
    i1 = jnp.min(jnp.where(el == m1, lane_f, big), axis=-1, keepdims=True)
    el2 = jnp.where(lane_f == i1, neg, el)
    m2 = jnp.max(el2, axis=-1, keepdims=True)
    i2 = jnp.min(jnp.where(el2 == m2, lane_f, big), axis=-1, keepdims=True)
    z = jnp.sum(jnp.exp(el - m1), axis=-1, keepdims=True)
    p1 = 1.0 / z
    p2 = jnp.exp(m2 - m1) / z
    psum = p1 + p2
    e1 = i1 - float(MOE_GROUPS)
    e2 = i2 - float(MOE_GROUPS)

    tm = logits.shape[0]
    oh1 = (lane_f == e1).astype(F32)
    oh2 = (lane_f == e2).astype(F32)
    oh = oh1 + oh2
    row = lax.broadcasted_iota(jnp.int32, (tm, tm), 0)
    col = lax.broadcasted_iota(jnp.int32, (tm, tm), 1)
    before = _dot((row > col).astype(BF16), oh.astype(BF16)) + cnt
    r1 = jnp.sum(oh1 * before, axis=-1, keepdims=True)
    r2 = jnp.sum(oh2 * before, axis=-1, keepdims=True)
    cnt = cnt + jnp.sum(oh, axis=0, keepdims=True)

    meta = jnp.where(lane == 0, e1, jnp.where(lane == 1, e2, jnp.where(lane == 2, r1, jnp.where(lane == 3, r2, 0.0))))
    gates = jnp.where(lane == 0, g_w * (p1 / psum), jnp.where(lane == 1, g_w * (p2 / psum), 0.0))
    return hi, meta.astype(jnp.int32), gates, cnt


def _router_kernel(x_ref, g_ref, sh_ref, sc_ref, w_ref, b_ref, h_ref, id_ref, gt_ref, cnt_ref, cnt_scr):
    @pl.when(pl.program_id(0) == 0)
    def _():
        cnt_scr[...] = jnp.zeros_like(cnt_scr)

    h = _norm_mod(x_ref[...], g_ref[...], sh_ref[0], sc_ref[0])
    hi, meta, gates, cnt = _route(h, w_ref[...], b_ref[...], cnt_scr[...])
    h_ref[...] = hi.reshape(h_ref.shape)
    id_ref[...] = meta
    gt_ref[...] = gates
    cnt_scr[...] = cnt
    cnt_ref[...] = jnp.broadcast_to(cnt, cnt_ref.shape).astype(jnp.int32)


def _router(x, gain, shift, scale, w_grp, b_grp, w_exp, b_exp, rows_per_batch, tm=512):
    t, d = x.shape
    bsz = shift.shape[0]
    tm = _pick(rows_per_batch, tm)
    tpb = rows_per_batch // tm
    n_log = MOE_GROUPS + MOE_EXPERTS
    wcat = jnp.zeros((d, LANES), F32).at[:, :MOE_GROUPS].set(w_grp).at[:, MOE_GROUPS:n_log].set(w_exp)
    w_hi = wcat.astype(BF16)
    w_lo = (wcat - w_hi.astype(F32)).astype(BF16)
    w2 = jnp.concatenate([w_hi, w_lo], axis=1)
    bcat = jnp.zeros((1, LANES), F32).at[0, :MOE_GROUPS].set(b_grp).at[0, MOE_GROUPS:n_log].set(b_exp)
    return pl.pallas_call(
        _router_kernel,
        grid=(t // tm,),
        in_specs=[pl.BlockSpec((tm, d), lambda i: (i, 0)),
                  pl.BlockSpec((1, d), lambda i: (0, 0)),
                  pl.BlockSpec((1, 1, d), lambda i: (i // tpb, 0, 0)),
                  pl.BlockSpec((1, 1, d), lambda i: (i // tpb, 0, 0)),
                  pl.BlockSpec((d, 2 * LANES), lambda i: (0, 0)),
                  pl.BlockSpec((1, LANES), lambda i: (0, 0))],
        out_specs=[pl.BlockSpec((tm, d // LANES, LANES), lambda i: (i, 0, 0)),
                   pl.BlockSpec((tm, LANES), lambda i: (i, 0)),
                   pl.BlockSpec((tm, LANES), lambda i: (i, 0)),
                   pl.BlockSpec((8, LANES), lambda i: (0, 0))],
        out_shape=[jax.ShapeDtypeStruct((t, d // LANES, LANES), BF16),
                   jax.ShapeDtypeStruct((t, LANES), jnp.int32),
                   jax.ShapeDtypeStruct((t, LANES), F32),
                   jax.ShapeDtypeStruct((8, LANES), jnp.int32)],
        scratch_shapes=[pltpu.VMEM((1, LANES), F32)],
        compiler_params=_params("arbitrary"),
        name="moe_router",
    )(x, gain.reshape(1, d), shift.reshape(bsz, 1, d), scale.reshape(bsz, 1, d), w2, bcat)


def _dispatch_tables(ids, rank, counts, n_tiles, tile):
    blk = MOE_BLOCK
    a = ids.shape[0] * MOE_TOPK
    padded = (counts + blk - 1) // blk * blk
    pad_ends = jnp.cumsum(padded)
    pad_starts = pad_ends - padded
    onehot = ids[:, :, None] == jnp.arange(MOE_EXPERTS, dtype=jnp.int32)
    dest = jnp.sum(jnp.where(onehot, pad_starts, 0), axis=-1) + rank
    dest_tiles = dest.reshape(n_tiles, tile, MOE_TOPK).transpose(0, 2, 1).reshape(n_tiles, 1, MOE_TOPK * tile)
    nb = -(-(a + MOE_EXPERTS * (blk - 1)) // (blk * MOE_BLOCKS_PER_STEP)) * MOE_BLOCKS_PER_STEP
    block_start = jnp.arange(nb, dtype=jnp.int32) * blk
    block_exp = jnp.minimum(jnp.sum(pad_ends[None, :] <= block_start[:, None], axis=1), MOE_EXPERTS - 1)
    n_used = (pad_ends[-1] // blk).reshape(1)
    fill = jnp.stack([pad_starts + counts, padded - counts,
                      jnp.broadcast_to(n_used, (MOE_EXPERTS,))])
    first = jnp.concatenate([jnp.ones((1,), bool), block_exp[1:] != block_exp[:-1]])
    slot = (jnp.cumsum(first) - 1) % 2
    e_ids = jnp.arange(MOE_EXPERTS, dtype=jnp.int32)
    later = jnp.where((counts > 0)[None, :] & (e_ids[None, :] > e_ids[:, None]), e_ids[None, :], MOE_EXPERTS)
    nxt = jnp.min(later, axis=1)
    nxt = jnp.where(nxt < MOE_EXPERTS, nxt, -1)[block_exp]
    sched = jnp.stack([block_exp, first.astype(jnp.int32), slot, nxt])
    return (dest.astype(jnp.int32), dest_tiles.astype(jnp.int32), sched.astype(jnp.int32),
            n_used.astype(jnp.int32), fill.astype(jnp.int32))


def _dispatch_kernel(fill_ref, dest_ref, h_ref, xs_hbm, zbuf, sem, zsem):
    tm = h_ref.shape[0]
    i = pl.program_id(0)

    def fill_copies(start):
        def issue(cp, on):
            @pl.when(on)
            def _():
                cp.start() if start else cp.wait()

        for e in range(MOE_EXPERTS):
            off = fill_ref[0, e]
            n_pad = fill_ref[1, e]
            bit = MOE_BLOCK // 2
            while bit:
                issue(pltpu.make_async_copy(zbuf.at[pl.ds(0, bit)], xs_hbm.at[pl.ds(off, bit)], zsem),
                      (n_pad & bit) != 0)
                off = off + (n_pad & bit)
                bit //= 2

        half = zbuf.shape[0]
        first = fill_ref[2, 0] * (MOE_BLOCK // half)

        def tail(j, c):
            cp = pltpu.make_async_copy(zbuf, xs_hbm.at[pl.ds(j * half, half)], zsem)
            cp.start() if start else cp.wait()
            return c
        lax.fori_loop(first, xs_hbm.shape[0] // half, tail, 0)

    @pl.when(i == 0)
    def _():
        zbuf[...] = jnp.zeros_like(zbuf)
        fill_copies(True)
        fill_copies(False)

    def row_copy(j, dst):
        return pltpu.make_async_copy(h_ref.at[pl.ds(j // MOE_TOPK, 1)], xs_hbm.at[pl.ds(dst, 1)], sem)

    for j in range(MOE_TOPK * tm):
        row_copy(j, dest_ref[0, 0, j]).start(priority=j % 2)

    def wait(j, c):
        row_copy(0, 0).wait()
        return c
    lax.fori_loop(0, MOE_TOPK * tm, wait, 0, unroll=8)


def _dispatch(h2, dest, fill, n_rows):
    t, sub, lanes = h2.shape
    tm = _pick(t, MOE_DISPATCH_TILE)
    nt, two_tm = t // tm, MOE_TOPK * tm
    dest_tiles = dest.reshape(nt, 1, two_tm)
    grid_spec = pltpu.PrefetchScalarGridSpec(
        num_scalar_prefetch=1,
        grid=(nt,),
        in_specs=[pl.BlockSpec((1, 1, two_tm), lambda i, f: (i, 0, 0), memory_space=pltpu.SMEM),
                  pl.BlockSpec((tm, sub, lanes), lambda i, f: (i, 0, 0))],
        out_specs=pl.BlockSpec(memory_space=pl.ANY),
        scratch_shapes=[pltpu.VMEM((MOE_BLOCK // 2, sub, lanes), BF16),
                        pltpu.SemaphoreType.DMA(()),
                        pltpu.SemaphoreType.DMA(())],
    )
    return pl.pallas_call(
        _dispatch_kernel,
        grid_spec=grid_spec,
        out_shape=jax.ShapeDtypeStruct((n_rows, sub, lanes), BF16),
        compiler_params=_params("arbitrary"),
        name="moe_dispatch",
    )(fill, dest_tiles, h2)


def _experts_kernel(sched_ref, nused_ref, x_ref, w1_hbm, w3_hbm, w2_hbm, y_ref,
                    w1f, w3f, w2f, w1b, w3b, w2b, wsem, *, layer):
    blk = MOE_BLOCK
    _, sub, lanes = x_ref.shape

    def weight_copies(ex, s):
        return [pltpu.make_async_copy(src.at[layer, ex], dst.at[s], wsem.at[s])
                for src, dst in ((w1_hbm, w1f), (w3_hbm, w3f), (w2_hbm, w2f))]

    @pl.when(pl.program_id(0) == 0)
    def _():
        for cp in weight_copies(sched_ref[0, 0], 0):
            cp.start()

    for piece in range(x_ref.shape[0] // blk):
        b = pl.program_id(0) * (x_ref.shape[0] // blk) + piece
        rows = pl.ds(piece * blk, blk)
        used = b < nused_ref[0]
        e, first, slot, e_next = (sched_ref[r, b] for r in range(4))

        @pl.when(used & (first == 1))
        def _():
            for cp in weight_copies(e, slot):
                cp.wait()

            @pl.when(e_next >= 0)
            def _():
                for cp in weight_copies(e_next, 1 - slot):
                    cp.start()

            w1b[...] = w1f[slot].astype(BF16)
            w3b[...] = w3f[slot].astype(BF16)
            w2b[...] = w2f[slot].astype(BF16)

        @pl.when(used)
        def _():
            x = x_ref[rows].reshape(blk, sub * lanes)
            a = _dot(x, w1b[...])
            hb = (a * _sigmoid(a)) * _dot(x, w3b[...])
            y_ref[rows] = _dot(hb.astype(BF16), w2b[...]).astype(BF16).reshape(blk, sub, lanes)

        @pl.when(jnp.logical_not(used))
        def _():
            y_ref[rows] = jnp.zeros((blk, sub, lanes), y_ref.dtype)


def _experts(xs, sched, n_used, w1, w3, w2, layer):
    nb = sched.shape[1]
    _, sub, lanes = xs.shape
    d = sub * lanes
    hid = w1.shape[-1]
    blk = MOE_BLOCK
    per = MOE_BLOCKS_PER_STEP
    assert nb % per == 0

    def xmap(i, sc, nu):
        return (jnp.minimum(i, (jnp.maximum(nu[0], 1) + per - 1) // per - 1), 0, 0)

    grid_spec = pltpu.PrefetchScalarGridSpec(
        num_scalar_prefetch=2,
        grid=(nb // per,),
        in_specs=[pl.BlockSpec((per * blk, sub, lanes), xmap),
                  pl.BlockSpec(memory_space=pl.ANY),
                  pl.BlockSpec(memory_space=pl.ANY),
                  pl.BlockSpec(memory_space=pl.ANY)],
        out_specs=pl.BlockSpec((per * blk, sub, lanes), lambda i, sc, nu: (i, 0, 0)),
        scratch_shapes=[pltpu.VMEM((2, d, hid), F32),
                        pltpu.VMEM((2, d, hid), F32),
                        pltpu.VMEM((2, hid, d), F32),
                        pltpu.VMEM((d, hid), BF16),
                        pltpu.VMEM((d, hid), BF16),
                        pltpu.VMEM((hid, d), BF16),
                        pltpu.SemaphoreType.DMA((2,))],
    )
    return pl.pallas_call(
        functools.partial(_experts_kernel, layer=layer),
        grid_spec=grid_spec,
        out_shape=jax.ShapeDtypeStruct((nb * blk, sub, lanes), BF16),
        compiler_params=_params("arbitrary"),
        name="moe_experts",
    )(sched, n_used, xs, w1, w3, w2)


def _combine_kernel(*refs, final):
    nd = COMBINE_TILES + COMBINE_AHEAD
    dst_refs = refs[:nd]
    x_ref, gt_ref, gate_ref, post_ref, sh_ref, sc_ref, ys_hbm, o_ref = refs[nd:nd + 8]
    rest = refs[nd + 8:]
    hn_ref = None if final else rest[0]
    ybuf, sem = rest[-2:]
    tm = x_ref.shape[0] // COMBINE_TILES
    d = x_ref.shape[1]
    n_rows = MOE_TOPK * tm
    i = pl.program_id(0)

    def row_copy(row, j, s):
        return pltpu.make_async_copy(ys_hbm.at[pl.ds(row, 1)], ybuf.at[s, pl.ds(j, 1)], sem.at[s])

    def start_rows(dst_ref, s, lo, hi):
        for j in range(lo, hi):
            row_copy(dst_ref[0, 0, j], j, s).start(priority=j % 2)

    def wait_all(s):
        def body(j, c):
            row_copy(0, 0, s).wait()
            return c
        lax.fori_loop(0, n_rows, body, 0, unroll=8)

    def tile(s):
        n_pieces = tm // COMBINE_PIECE_ROWS
        per = n_rows // n_pieces
        for p in range(n_pieces):
            start_rows(dst_refs[s + COMBINE_AHEAD], (s + COMBINE_AHEAD) % COMBINE_TILES, p * per, (p + 1) * per)
            lo = p * COMBINE_PIECE_ROWS
            rs = pl.ds(s * tm + lo, COMBINE_PIECE_ROWS)
            gt = gt_ref[rs, :]
            y0 = ybuf[s, pl.ds(lo, COMBINE_PIECE_ROWS)].reshape(COMBINE_PIECE_ROWS, d).astype(F32)
            y1 = ybuf[s, pl.ds(tm + lo, COMBINE_PIECE_ROWS)].reshape(COMBINE_PIECE_ROWS, d).astype(F32)
            x = x_ref[rs, :] + gate_ref[0] * (gt[:, 0:1] * y0 + gt[:, 1:2] * y1)
            if final:
                x = x * lax.rsqrt(jnp.mean(x * x, axis=-1, keepdims=True) + EPS) * post_ref[...]
            else:
                hn_ref[rs, :] = _norm_mod(x, post_ref[...], sh_ref[0], sc_ref[0]).astype(BF16)
            o_ref[rs, :] = x

    @pl.when(i == 0)
    def _():
        for s in range(COMBINE_AHEAD):
            start_rows(dst_refs[s], s, 0, n_rows)

    for s in range(COMBINE_TILES):
        wait_all(s)
        tile(s)

    @pl.when(i == pl.num_programs(0) - 1)
    def _():
        for s in range(COMBINE_AHEAD):
            wait_all(s)


def _combine(x, ys, dest_tiles, gates, gate, post_gain, post_shift, post_scale, rows_per_batch, final):
    t, d = x.shape
    bsz = gate.shape[0]
    nt, _, two_tm = dest_tiles.shape
    tm = two_tm // MOE_TOPK
    tpb = rows_per_batch // tm
    k = COMBINE_TILES
    assert nt % k == 0 and tpb % k == 0 and COMBINE_AHEAD < k
    smem = functools.partial(pl.BlockSpec, memory_space=pltpu.SMEM)
    per_batch = lambda i: (k * i // tpb, 0, 0)
    rows = pl.BlockSpec((k * tm, d), lambda i: (i, 0))
    out_specs, out_shape = [rows], [jax.ShapeDtypeStruct((t, d), F32)]
    if not final:
        out_specs.append(rows)
        out_shape.append(jax.ShapeDtypeStruct((t, d), BF16))
    n_dst = k + COMBINE_AHEAD
    return pl.pallas_call(
        functools.partial(_combine_kernel, final=final),
        grid=(nt // k,),
        in_specs=[smem((1, 1, two_tm), lambda i, s=s: ((k * i + s) % nt, 0, 0)) for s in range(n_dst)] + [
                  rows,
                  pl.BlockSpec((k * tm, LANES), lambda i: (i, 0)),
                  pl.BlockSpec((1, 1, d), per_batch),
                  pl.BlockSpec((1, d), lambda i: (0, 0)),
                  pl.BlockSpec((1, 1, d), per_batch),
                  pl.BlockSpec((1, 1, d), per_batch),
                  pl.BlockSpec(memory_space=pl.ANY)],
        out_specs=out_specs,
        out_shape=out_shape,
        scratch_shapes=[pltpu.VMEM((k, two_tm) + ys.shape[1:], BF16),
                        pltpu.SemaphoreType.DMA((k,))],
        compiler_params=_params("arbitrary"),
        name="moe_combine",
    )(*([dest_tiles] * n_dst), x, gates, gate.reshape(bsz, 1, d), post_gain.reshape(1, d),
      post_shift.reshape(bsz, 1, d), post_scale.reshape(bsz, 1, d), ys)


def _moe(x, routed, gate, w1, w3, w2, layer, post, seq, final):
    t, _ = x.shape
    tile = _pick(seq, MOE_TOKEN_TILE)
    h2, meta, gts, cnt = routed
    dest, dest_tiles, sched, n_used, fill = _dispatch_tables(
        meta[:, :MOE_TOPK], meta[:, MOE_TOPK:2 * MOE_TOPK], cnt[0, :MOE_EXPERTS], t // tile, tile)
    xs = _dispatch(h2, dest, fill, sched.shape[1] * MOE_BLOCK)
    ys = _experts(xs, sched, n_used, w1, w3, w2, layer)
    return _combine(x, ys, dest_tiles, gts, gate, *post, seq, final)


def kernel(x, c, norm_mix, norm_ffn, ada_w, ada_b, gla_w_in, gla_b_r, gla_w_a_up, gla_b_a, gla_norm_g, gla_w_out, sgu_w_in, sgu_b_in, sgu_ln_g, sgu_ln_b, sgu_w_s, sgu_b_s, sgu_w_out, moe_w_grp, moe_b_grp, moe_w_exp, moe_b_exp, moe_w1, moe_w3, moe_w2, final_norm):
    bsz, seq, d = x.shape
    depth = norm_mix.shape[0]
    xt = x.reshape(bsz * seq, d)
    mod = _adaln(c, ada_w, ada_b)
    mods = [[mod[i, :, m * d:(m + 1) * d] for m in range(N_MOD)] for i in range(depth)]
    mixer_in = xt
    for i in range(depth):
        sh1, sc1, g1, sh2, sc2, g2 = mods[i]
        j = i // 2
        if i % 2 == 0:
            w_in = gla_w_in[j]
            n_main = w_in.shape[1] - GLA_RANK
            w_main = w_in[:, :n_main].astype(BF16)
            w_low = jnp.zeros((d, LANES), F32).at[:, :GLA_RANK].set(w_in[:, n_main:]).astype(BF16)
            proj, a_low = _norm_matmul(mixer_in, norm_mix[i], sh1, sc1, w_main, seq, "gla_in_proj",
                                       w_side=w_low, tn=2048)
            o = _gla_scan(proj, a_low, gla_w_a_up[j], gla_b_a[j], gla_b_r[j], gla_norm_g[j], bsz, seq)
            xt = _matmul_residual(o, gla_w_out[j].astype(BF16), xt, g1, seq, "gla_out_proj", tm=1024)
        else:
            z, = _norm_matmul(mixer_in, norm_mix[i], sh1, sc1, sgu_w_in[j].astype(BF16), seq, "sgu_in_proj",
                              bias=sgu_b_in[j], tm=2048, tn=1024)
            o = _sgu_spatial(z, sgu_ln_g[j], sgu_ln_b[j], sgu_w_s[j], sgu_b_s[j])
            xt = _matmul_residual(o, sgu_w_out[j].astype(BF16), xt, g1, seq, "sgu_out_proj")
        routed = _router(xt, norm_ffn[i], sh2, sc2, moe_w_grp[i], moe_b_grp[i], moe_w_exp[i], moe_b_exp[i], seq)
        final = i == depth - 1
        post = (final_norm, sh2, sc2) if final else (norm_mix[i + 1], mods[i + 1][0], mods[i + 1][1])
        xt, *nxt = _moe(xt, routed, g2, moe_w1, moe_w3, moe_w2, i, post, seq, final)
        mixer_in = xt if final else nxt[0]
    return xt.reshape(bsz, seq, d)
```
